```python
import jax, jax.numpy as jnp
from jax import lax
import numpy as np

D_MODEL = 1024
BATCH = 4
SEQ = 8192
DEPTH = 4

GRID_W = 64
CTX_LEN = 256
HEAD_DIM = 64
NORM_EPS = 1e-6
NA_HEADS = 8
NA_WIN_ROWS = 8
NA_WIN_COLS = 16
NA_W = NA_HEADS * HEAD_DIM
SW_HEADS = 8
SW_KV_HEADS = 2
SW_WINDOW = 128
SW_BLOCK = 128
SW_QW = SW_HEADS * HEAD_DIM
SW_KVW = SW_KV_HEADS * HEAD_DIM
ROPE_THETA = 10000.0
ROPE_FREQS = HEAD_DIM // 4
CONV_WIDTH = 512
CONV_K = 3
D_FF = -(-8 * D_MODEL // (3 * 256)) * 256
N_BRANCHES = 3
IN_SPLITS = (NA_W, NA_W, NA_W, SW_QW, SW_KVW, SW_KVW, CONV_WIDTH, CONV_WIDTH, CONV_WIDTH, N_BRANCHES * D_MODEL)
N_IN = sum(IN_SPLITS)

kernel_name = "hybrid_dit_natten_swa_shortconv"


def rms_norm(x, g):
    xf = x.astype(jnp.float32)
    y = xf * lax.rsqrt(jnp.mean(xf * xf, axis=-1, keepdims=True) + NORM_EPS)
    return (y * g.astype(jnp.float32)).astype(x.dtype)


def modulate(h, shift, scale):
    return h * (1 + scale) + shift


def split_in(z):
    idx = np.cumsum(IN_SPLITS)[:-1].tolist()
    return jnp.split(z, idx, axis=-1)


def to_heads(t, n_heads):
    b, n, _ = t.shape
    return t.reshape(b, n, n_heads, HEAD_DIM).transpose(0, 2, 1, 3)


def from_heads(o):
    b, h, n, d = o.shape
    return o.transpose(0, 2, 1, 3).reshape(b, n, h * d)


def axial_rope(n_tokens, dtype):
    t = jnp.arange(n_tokens, dtype=jnp.int32)
    row = (t // GRID_W).astype(jnp.float32)
    col = (t % GRID_W).astype(jnp.float32)
    inv = ROPE_THETA ** (-jnp.arange(ROPE_FREQS, dtype=jnp.float32) / ROPE_FREQS)
    ang = jnp.concatenate([row[:, None] * inv, col[:, None] * inv], axis=-1)
    return jnp.cos(ang).astype(dtype), jnp.sin(ang).astype(dtype)


def apply_rope(x, cos, sin):
    x1, x2 = x[..., :HEAD_DIM // 2], x[..., HEAD_DIM // 2:]
    return jnp.concatenate([x1 * cos - x2 * sin, x1 * sin + x2 * cos], axis=-1)


def softmax_sink(s, sink):
    if sink is None:
        return jax.nn.softmax(s, axis=-1)
    m = jnp.maximum(jnp.max(s, axis=-1, keepdims=True), sink)
    e = jnp.exp(s - m)
    return e / (jnp.sum(e, axis=-1, keepdims=True) + jnp.exp(sink - m))


def context_attention(qc, kc, vc, sink):
    b, hq, n, d = qc.shape
    hkv = kc.shape[1]
    rep = hq // hkv
    qg = qc.reshape(b, hkv, rep, n, d)
    s = jnp.einsum('bgrqd,bgkd->bgrqk', qg, kc).astype(jnp.float32) * (HEAD_DIM ** -0.5)
    sink_g = None if sink is None else sink.reshape(hkv, rep)[None, :, :, None, None].astype(jnp.float32)
    p = softmax_sink(s, sink_g).astype(vc.dtype)
    return jnp.einsum('bgrqk,bgkd->bgrqd', p, vc).reshape(b, hq, n, d)


def neighbourhood_attention(q, k, v, kc, vc, rpb, rows):
    b, h, s_len, d = q.shape
    kh = min(NA_WIN_ROWS, rows)
    kw = NA_WIN_COLS
    scale = HEAD_DIM ** -0.5
    qg = q.reshape(b, h, rows, GRID_W, d)
    kg_all = k.reshape(b, h, rows, GRID_W, d)
    vg_all = v.reshape(b, h, rows, GRID_W, d)
    col = jnp.arange(GRID_W)
    col_start = jnp.clip(col - kw // 2, 0, GRID_W - kw)
    col_idx = col_start[:, None] + jnp.arange(kw)[None, :]
    dcol = col_idx - col[:, None] + (NA_WIN_COLS - 1)
    rpb = rpb.astype(jnp.float32)

    def row_block(r):
        r0 = jnp.clip(r - kh // 2, 0, rows - kh)
        qr = lax.dynamic_index_in_dim(qg, r, axis=2, keepdims=False)
        k_rows = lax.dynamic_slice_in_dim(kg_all, r0, kh, axis=2)
        v_rows = lax.dynamic_slice_in_dim(vg_all, r0, kh, axis=2)
        kgat = k_rows[:, :, :, col_idx, :]
        vgat = v_rows[:, :, :, col_idx, :]
        s_nb = jnp.einsum('bhwd,bhiwjd->bhwij', qr, kgat).astype(jnp.float32) * scale
        drow = r0 + jnp.arange(kh) - r + (NA_WIN_ROWS - 1)
        bias = rpb[:, drow[:, None, None], dcol[None, :, :]]
        s_nb = (s_nb + bias.transpose(0, 2, 1, 3)[None]).reshape(b, h, GRID_W, kh * kw)
        s_cx = jnp.einsum('bhwd,bhcd->bhwc', qr, kc).astype(jnp.float32) * scale
        p = jax.nn.softmax(jnp.concatenate([s_nb, s_cx], axis=-1), axis=-1).astype(v.dtype)
        p_nb = p[..., :kh * kw].reshape(b, h, GRID_W, kh, kw)
        p_cx = p[..., kh * kw:]
        return (jnp.einsum('bhwij,bhiwjd->bhwd', p_nb, vgat)
                + jnp.einsum('bhwc,bhcd->bhwd', p_cx, vc))

    out = lax.map(row_block, jnp.arange(rows))
    return out.transpose(1, 2, 0, 3, 4).reshape(b, h, s_len, d)


def sliding_window_gqa(q, k, v, kc, vc, sink):
    b, hq, s_len, d = q.shape
    hkv = k.shape[1]
    rep = hq // hkv
    nb = s_len // SW_BLOCK
    band = 3 * SW_BLOCK
    scale = HEAD_DIM ** -0.5
    qg = q.reshape(b, hkv, rep, s_len, d)
    pad = ((0, 0), (0, 0), (SW_BLOCK, SW_BLOCK), (0, 0))
    kp = jnp.pad(k, pad)
    vp = jnp.pad(v, pad)
    rel = (jnp.arange(band) - SW_BLOCK)[None, :] - jnp.arange(SW_BLOCK)[:, None]
    in_window = jnp.abs(rel) <= SW_WINDOW
    sink_g = sink.reshape(hkv, rep)[None, :, :, None, None].astype(jnp.float32)

    def block(n):
        q0 = n * SW_BLOCK
        qb = lax.dynamic_slice_in_dim(qg, q0, SW_BLOCK, axis=3)
        kb = lax.dynamic_slice_in_dim(kp, q0, band, axis=2)
        vb = lax.dynamic_slice_in_dim(vp, q0, band, axis=2)
        kabs = q0 - SW_BLOCK + jnp.arange(band)
        mask = in_window & ((kabs >= 0) & (kabs < s_len))[None, :]
        s_loc = jnp.einsum('bgrqd,bgkd->bgrqk', qb, kb).astype(jnp.float32) * scale
        s_loc = jnp.where(mask, s_loc, -jnp.inf)
        s_cx = jnp.einsum('bgrqd,bgcd->bgrqc', qb, kc).astype(jnp.float32) * scale
        p = softmax_sink(jnp.concatenate([s_loc, s_cx], axis=-1), sink_g).astype(v.dtype)
        return (jnp.einsum('bgrqk,bgkd->bgrqd', p[..., :band], vb)
                + jnp.einsum('bgrqc,bgcd->bgrqd', p[..., band:], vc))

    out = lax.map(block, jnp.arange(nb))
    return out.transpose(1, 2, 3, 0, 4, 5).reshape(b, hq, s_len, d)


def short_conv_mixer(xin, gate_b, gate_c, conv_w):
    u = gate_c * xin
    y = lax.conv_general_dilated(
        u, conv_w[:, None, :].astype(u.dtype), window_strides=(1,),
        padding=((CONV_K // 2, CONV_K // 2),), dimension_numbers=('NWC', 'WIO', 'NWC'),
        feature_group_count=u.shape[-1])
    return gate_b * y


def merge_branches(a_out, b_out, c_out, z_gate, b_gate_l, w_br_a_l, w_br_b_l, w_br_c_l, w_o_l):
    ga, gb, gc = jnp.split(jax.nn.sigmoid(z_gate + b_gate_l), N_BRANCHES, axis=-1)
    merged = ga * (a_out @ w_br_a_l) + gb * (b_out @ w_br_b_l) + gc * (c_out @ w_br_c_l)
    return merged @ w_o_l


def swiglu(h, w_up, w_down):
    gate, up = jnp.split(h @ w_up, 2, axis=-1)
    return (jax.nn.silu(gate) * up) @ w_down


def setup_inputs(seed: int = 0) -> dict:
    key = jax.random.key(seed)
    ks = jax.random.split(key, 20)
    f32 = jnp.float32

    def nrm(k, shape, scale):
        return jax.random.normal(k, shape, f32) * scale

    return {
        "x": nrm(ks[0], (BATCH, SEQ, D_MODEL), 1.0),
        "c": nrm(ks[1], (BATCH, D_MODEL), 1.0),
        "ctx": nrm(ks[2], (BATCH, CTX_LEN, D_MODEL), 1.0),
        "c_ctx": nrm(ks[3], (D_MODEL,), 1.0),
        "w_ada": nrm(ks[4], (DEPTH, D_MODEL, 6 * D_MODEL), 0.5 * D_MODEL ** -0.5),
        "b_ada": nrm(ks[5], (DEPTH, 6 * D_MODEL), 0.02),
        "norm1_g": 1.0 + nrm(ks[6], (DEPTH, D_MODEL), 0.02),
        "norm2_g": 1.0 + nrm(ks[7], (DEPTH, D_MODEL), 0.02),
        "w_in": nrm(ks[8], (DEPTH, D_MODEL, N_IN), D_MODEL ** -0.5),
        "b_gate": nrm(ks[9], (DEPTH, N_BRANCHES * D_MODEL), 0.02),
        "na_rpb": nrm(ks[10], (DEPTH, NA_HEADS, 2 * NA_WIN_ROWS - 1, 2 * NA_WIN_COLS - 1), 0.1),
        "sw_sink": nrm(ks[11], (DEPTH, SW_HEADS), 1.0),
        "conv_w": nrm(ks[12], (DEPTH, CONV_K, CONV_WIDTH), CONV_K ** -0.5),
        "w_br_a": nrm(ks[13], (DEPTH, NA_W, D_MODEL), NA_W ** -0.5),
        "w_br_b": nrm(ks[14], (DEPTH, SW_QW, D_MODEL), SW_QW ** -0.5),
        "w_br_c": nrm(ks[15], (DEPTH, CONV_WIDTH, D_MODEL), CONV_WIDTH ** -0.5),
        "w_o": nrm(ks[16], (DEPTH, D_MODEL, D_MODEL), D_MODEL ** -0.5),
        "w_ffn_up": nrm(ks[17], (DEPTH, D_MODEL, 2 * D_FF), D_MODEL ** -0.5),
        "w_ffn_down": nrm(ks[18], (DEPTH, D_FF, D_MODEL), D_FF ** -0.5),
        "final_norm_g": 1.0 + nrm(ks[19], (D_MODEL,), 0.02),
    }


def reference(x, c, ctx, c_ctx, w_ada, b_ada, norm1_g, norm2_g, w_in, b_gate, na_rpb, sw_sink,
              conv_w, w_br_a, w_br_b, w_br_c, w_o, w_ffn_up, w_ffn_down, final_norm_g):
    b, s_len, _ = x.shape
    rows = s_len // GRID_W
    cos, sin = axial_rope(s_len, x.dtype)
    silu_c = jax.nn.silu(c)
    silu_cc = jax.nn.silu(c_ctx)
    xc = ctx
    for l in range(DEPTH):
        last = l == DEPTH - 1
        sh1, sc1, g1, sh2, sc2, g2 = jnp.split((silu_c @ w_ada[l] + b_ada[l])[:, None, :], 6, axis=-1)
        csh1, csc1, cg1, csh2, csc2, cg2 = jnp.split(silu_cc @ w_ada[l] + b_ada[l], 6, axis=-1)

        h = modulate(rms_norm(x, norm1_g[l]), sh1, sc1)
        hc = modulate(rms_norm(xc, norm1_g[l]), csh1, csc1)
        qa, ka, va, qb, kb, vb, cx, cgb, cgc, zg = split_in(h @ w_in[l])
        qa_c, ka_c, va_c, qb_c, kb_c, vb_c, cx_c, cgb_c, cgc_c, zg_c = split_in(hc @ w_in[l])
        ka_c = to_heads(ka_c, NA_HEADS)
        va_c = to_heads(va_c, NA_HEADS)
        kb_c = to_heads(kb_c, SW_KV_HEADS)
        vb_c = to_heads(vb_c, SW_KV_HEADS)

        a_out = from_heads(neighbourhood_attention(
            to_heads(qa, NA_HEADS), to_heads(ka, NA_HEADS), to_heads(va, NA_HEADS),
            ka_c, va_c, na_rpb[l], rows))
        b_out = from_heads(sliding_window_gqa(
            apply_rope(to_heads(qb, SW_HEADS), cos, sin),
            apply_rope(to_heads(kb, SW_KV_HEADS), cos, sin),
            to_heads(vb, SW_KV_HEADS), kb_c, vb_c, sw_sink[l]))
        c_out = short_conv_mixer(cx, cgb, cgc, conv_w[l])
        x = x + g1 * merge_branches(a_out, b_out, c_out, zg, b_gate[l], w_br_a[l], w_br_b[l], w_br_c[l], w_o[l])

        if not last:
            a_c = from_heads(context_attention(to_heads(qa_c, NA_HEADS), ka_c, va_c, None))
            b_c = from_heads(context_attention(to_heads(qb_c, SW_HEADS), kb_c, vb_c, sw_sink[l]))
            c_c = short_conv_mixer(cx_c, cgb_c, cgc_c, conv_w[l])
            xc = xc + cg1 * merge_branches(a_c, b_c, c_c, zg_c, b_gate[l], w_br_a[l], w_br_b[l], w_br_c[l], w_o[l])

        h2 = modulate(rms_norm(x, norm2_g[l]), sh2, sc2)
        x = x + g2 * swiglu(h2, w_ffn_up[l], w_ffn_down[l])
        if not last:
            hc2 = modulate(rms_norm(xc, norm2_g[l]), csh2, csc2)
            xc = xc + cg2 * swiglu(hc2, w_ffn_up[l], w_ffn_down[l])

    return rms_norm(x, final_norm_g)
```

```python
import functools

import numpy as np
import jax
import jax.numpy as jnp
from jax import lax
from jax.experimental import pallas as pl
from jax.experimental.pallas import tpu as pltpu

F32 = jnp.float32
BF16 = jnp.bfloat16

D_MODEL = 1024
BATCH = 4
SEQ = 8192
DEPTH = 4
GRID_W = 64
GRID_ROWS = SEQ // GRID_W
CTX_LEN = 256
HEAD_DIM = 64
NORM_EPS = 1e-6
NA_HEADS = 8
NA_WIN_ROWS = 8
NA_WIN_COLS = 16
SW_HEADS = 8
SW_WINDOW = 128
ROPE_THETA = 10000.0
ROPE_FREQS = HEAD_DIM // 4
CONV_WIDTH = 512
D_FF = 2816
N_IN = 6912

N_LAT = BATCH * SEQ
N_CTX = BATCH * CTX_LEN
N_TOK = N_LAT + N_CTX

LANES = 128
NEG = -1e30

C_QA, C_KA, C_VA, C_QB, C_KB, C_VB, C_CX, C_CGB, C_CGC, C_ZG = (
    0, 512, 1024, 1536, 2048, 2176, 2304, 2816, 3328, 3840)

TM = 512
NA_QROWS = 2
NA_KROWS = 10
NA_TQ = NA_QROWS * GRID_W
NA_TK = NA_KROWS * GRID_W
NA_STEPS = GRID_ROWS // NA_QROWS
SW_TQ = 128
SW_TK = 3 * SW_TQ
SW_STEPS = SEQ // SW_TQ
CTX_STEPS = CTX_LEN // 128
VMEM_LIMIT = 56 * 1024 * 1024


def _cparams(n_axes):
    return pltpu.CompilerParams(dimension_semantics=("arbitrary",) * n_axes,
                                vmem_limit_bytes=VMEM_LIMIT)


def _const_spec(shape):
    nd = len(shape)
    return pl.BlockSpec(shape, lambda *_: (0,) * nd, pipeline_mode=pl.Buffered(1))


def _dot(a, b):
    return jnp.dot(a, b, preferred_element_type=F32)


def _dot_nt(a, b):
    return lax.dot_general(a, b, (((1,), (1,)), ((), ())), preferred_element_type=F32)


def _mod_norm(x, g, shift, scale):
    y = x * lax.rsqrt(jnp.mean(x * x, axis=-1, keepdims=True) + NORM_EPS) * g
    return y * (1.0 + scale) + shift


def _ada_kernel(c_ref, w_ref, b_ref, o_ref):
    c = c_ref[...]
    sc = c * jax.nn.sigmoid(c)
    o_ref[...] = jnp.dot(sc, w_ref[...], preferred_element_type=F32,
                         precision=lax.Precision.HIGHEST) + b_ref[...]


def _ada(cvecs, w_ada, b_ada):
    tn = 1536
    return pl.pallas_call(
        _ada_kernel,
        grid=(DEPTH, 6 * D_MODEL // tn),
        in_specs=[
            pl.BlockSpec((8, D_MODEL), lambda l, n: (0, 0)),
            pl.BlockSpec((None, D_MODEL, tn), lambda l, n: (l, 0, n)),
            pl.BlockSpec((None, 1, tn), lambda l, n: (l, 0, n)),
        ],
        out_specs=pl.BlockSpec((None, 8, tn), lambda l, n: (l, 0, n)),
        out_shape=jax.ShapeDtypeStruct((DEPTH, 8, 6 * D_MODEL), F32),
        compiler_params=_cparams(2),
        name="ada",
    )(cvecs, w_ada, b_ada.reshape(DEPTH, 1, 6 * D_MODEL))


def _mod_row(i):
    lat_blocks = N_LAT // TM
    return jnp.where(i < lat_blocks, i // (SEQ // TM), BATCH)


def _mod_spec(part):
    return pl.BlockSpec((None, 1, D_MODEL), lambda i: (_mod_row(i), 0, part))


def _inproj_kernel(x_ref, sh_ref, sc_ref, g_ref, w_ref, bg_ref, cos_ref, sin_ref,
                   qa_ref, ka_ref, va_ref, qb_ref, kb_ref, vb_ref, u_ref, cgb_ref, gate_ref):
    h = _mod_norm(x_ref[...], g_ref[...], sh_ref[...], sc_ref[...]).astype(BF16)

    def proj(c0, c1):
        return _dot(h, w_ref[:, c0:c1])

    qa_ref[...] = proj(C_QA, C_KA).astype(BF16)
    ka_ref[...] = proj(C_KA, C_VA).astype(BF16)
    va_ref[...] = proj(C_VA, C_QB).astype(BF16)

    cosv = cos_ref[...]
    sinv = sin_ref[...]

    def rope(t):
        return t * cosv + pltpu.roll(t, LANES // 2, 1) * sinv

    qb = proj(C_QB, C_KB)
    for p in range(4):
        qb_ref[:, LANES * p:LANES * (p + 1)] = rope(qb[:, LANES * p:LANES * (p + 1)]).astype(BF16)
    kv = proj(C_KB, C_CX)
    kb_ref[...] = rope(kv[:, :LANES]).astype(BF16)
    vb_ref[...] = kv[:, LANES:].astype(BF16)

    u_ref[...] = (proj(C_CGC, C_ZG) * proj(C_CX, C_CGB)).astype(BF16)
    cgb_ref[...] = proj(C_CGB, C_CGC).astype(BF16)
    for k in range(3):
        z = proj(C_ZG + D_MODEL * k, C_ZG + D_MODEL * (k + 1))
        gate_ref[:, D_MODEL * k:D_MODEL * (k + 1)] = jax.nn.sigmoid(
            z + bg_ref[:, D_MODEL * k:D_MODEL * (k + 1)]).astype(BF16)


def _inproj(xf, mods, g, w, bg, rope_c, rope_s):
    lat_blocks = N_LAT // TM
    seq_blocks = SEQ // TM

    def rope_idx(i):
        return (jnp.where(i < lat_blocks, i % seq_blocks, seq_blocks), 0)

    widths = (512, 512, 512, 512, 128, 128, 512, 512, 3072)
    return pl.pallas_call(
        _inproj_kernel,
        grid=(N_TOK // TM,),
        in_specs=[
            pl.BlockSpec((TM, D_MODEL), lambda i: (i, 0)),
            _mod_spec(0), _mod_spec(1),
            _const_spec((1, D_MODEL)),
            _const_spec((D_MODEL, N_IN)),
            _const_spec((1, 3 * D_MODEL)),
            pl.BlockSpec((TM, LANES), rope_idx),
            pl.BlockSpec((TM, LANES), rope_idx),
        ],
        out_specs=[pl.BlockSpec((TM, wd), lambda i: (i, 0)) for wd in widths],
        out_shape=[jax.ShapeDtypeStruct((N_TOK, wd), BF16) for wd in widths],
        compiler_params=_cparams(1),
        name="inproj",
    )(xf, mods, mods, g, w, bg, rope_c, rope_s)


def _softmax_pv(parts, sink=None):
    m = None
    for s, _ in parts:
        mx = jnp.max(s, axis=-1, keepdims=True)
        m = mx if m is None else jnp.maximum(m, mx)
    if sink is not None:
        m = jnp.maximum(m, sink)
    den = None
    acc = None
    for s, v in parts:
        e = jnp.exp(s - m)
        se = jnp.sum(e, axis=-1, keepdims=True)
        den = se if den is None else den + se
        pv = _dot(e.astype(BF16), v)
        acc = pv if acc is None else acc + pv
    if sink is not None:
        den = den + jnp.exp(sink - m)
    return acc / den


def _lane_iota():
    return lax.broadcasted_iota(jnp.int32, (1, LANES), 1)


def _na_kernel(q_ref, k_ref, v_ref, kc_ref, vc_ref, bt_ref, o_ref, bias_scr):
    j = pl.program_id(1)
    low_half = _lane_iota() < HEAD_DIM

    def head_pair(p, nb):
        sl = slice(LANES * p, LANES * (p + 1))
        qp = q_ref[:, sl]
        kcw = kc_ref[:, sl]
        vcw = vc_ref[:, sl]
        outs = []
        for hh in range(2):
            qm = jnp.where(low_half if hh == 0 else ~low_half, qp, jnp.zeros_like(qp))
            parts = []
            if nb is not None:
                kw, vw = nb(sl)
                parts.append((_dot_nt(qm, kw) + bias_scr[2 * p + hh], vw))
            parts.append((_dot_nt(qm, kcw), vcw))
            outs.append(_softmax_pv(parts))
        o_ref[:, sl] = jnp.where(low_half, outs[0], outs[1]).astype(o_ref.dtype)

    @pl.when(j < NA_STEPS)
    def _latent():
        kr0 = jnp.clip(NA_QROWS * j - NA_WIN_ROWS // 2, 0, GRID_ROWS - NA_KROWS)

        @pl.when((j <= 2) | (j >= NA_STEPS - 2))
        def _regen():
            for i in range(NA_QROWS):
                qr = NA_QROWS * j + i
                r0 = jnp.clip(qr - NA_WIN_ROWS // 2, 0, GRID_ROWS - NA_WIN_ROWS)
                for jj in range(0, NA_KROWS, 2):
                    tiles = []
                    for kk in (jj, jj + 1):
                        kr = kr0 + kk
                        valid = (kr >= r0) & (kr < r0 + NA_WIN_ROWS)
                        idx = jnp.clip(kr - qr + NA_WIN_ROWS - 1, 0, 2 * NA_WIN_ROWS - 2)
                        tiles.append(jnp.where(valid, bt_ref[idx], NEG))
                    bias_scr[:, GRID_W * i:GRID_W * (i + 1), GRID_W * jj:GRID_W * (jj + 2)] = (
                        jnp.concatenate(tiles, axis=-1))

        kstart = pl.multiple_of(kr0 * GRID_W, GRID_W)

        def nb(sl):
            return k_ref[pl.ds(kstart, NA_TK), sl], v_ref[pl.ds(kstart, NA_TK), sl]

        for p in range(NA_HEADS // 2):
            head_pair(p, nb)

    @pl.when(j >= NA_STEPS)
    def _context():
        for p in range(NA_HEADS // 2):
            head_pair(p, None)


def _attn_row_block(b, j, lat_steps, tq):
    lat = b * lat_steps + j
    ctx = N_LAT // tq + b * (CTX_LEN // tq) + (j - lat_steps)
    return jnp.where(j < lat_steps, lat, ctx)


def _na_attention(qa, ka, va, btab, with_ctx):
    steps = NA_STEPS + (CTX_STEPS if with_ctx else 0)
    width = NA_HEADS * HEAD_DIM
    n_out = N_TOK if with_ctx else N_LAT

    def q_idx(b, j):
        return (_attn_row_block(b, j, NA_STEPS, NA_TQ), 0)

    return pl.pallas_call(
        _na_kernel,
        grid=(BATCH, steps),
        in_specs=[
            pl.BlockSpec((NA_TQ, width), q_idx),
            pl.BlockSpec((SEQ, width), lambda b, j: (b, 0)),
            pl.BlockSpec((SEQ, width), lambda b, j: (b, 0)),
            pl.BlockSpec((CTX_LEN, width), lambda b, j: (N_LAT // CTX_LEN + b, 0)),
            pl.BlockSpec((CTX_LEN, width), lambda b, j: (N_LAT // CTX_LEN + b, 0)),
            _const_spec((2 * NA_WIN_ROWS - 1, NA_HEADS, GRID_W, GRID_W)),
        ],
        out_specs=pl.BlockSpec((NA_TQ, width), q_idx),
        out_shape=jax.ShapeDtypeStruct((n_out, width), BF16),
        scratch_shapes=[pltpu.VMEM((NA_HEADS, NA_TQ, NA_TK), F32)],
        compiler_params=_cparams(2),
        name="na_attn",
    )(qa, ka, va, ka, va, btab)


def _swa_kernel(sink_ref, q_ref, k_ref, v_ref, kc_ref, vc_ref, o_ref):
    j = pl.program_id(1)
    lane = _lane_iota()
    low_half = lane < HEAD_DIM
    group0 = (lane % HEAD_DIM) < HEAD_DIM // 2
    kcw = kc_ref[...]
    vcw = vc_ref[...]

    def head_pair(p, nb):
        sl = slice(LANES * p, LANES * (p + 1))
        qp = q_ref[:, sl]
        outs = []
        for g in range(2):
            qm = jnp.where(group0 if g == 0 else ~group0, qp, jnp.zeros_like(qp))
            parts = []
            if nb is not None:
                kw, vw, valid = nb
                parts.append((jnp.where(valid, _dot_nt(qm, kw), NEG), vw))
            parts.append((_dot_nt(qm, kcw), vcw))
            outs.append(_softmax_pv(parts, sink=sink_ref[p + (SW_HEADS // 2) * g]))
        o_ref[:, sl] = jnp.where(low_half, outs[0], outs[1]).astype(o_ref.dtype)

    @pl.when(j < SW_STEPS)
    def _latent():
        q0 = j * SW_TQ
        ks = pl.multiple_of(jnp.clip(q0 - SW_TQ, 0, SEQ - SW_TK), SW_TQ)
        kw = k_ref[pl.ds(ks, SW_TK), :]
        vw = v_ref[pl.ds(ks, SW_TK), :]
        rel = ((ks - q0) + lax.broadcasted_iota(jnp.int32, (SW_TQ, SW_TK), 1)
               - lax.broadcasted_iota(jnp.int32, (SW_TQ, SW_TK), 0))
        valid = jnp.abs(rel) <= SW_WINDOW
        for p in range(SW_HEADS // 2):
            head_pair(p, (kw, vw, valid))

    @pl.when(j >= SW_STEPS)
    def _context():
        for p in range(SW_HEADS // 2):
            head_pair(p, None)


def _swa_attention(sink, qb, kb, vb, with_ctx):
    steps = SW_STEPS + (CTX_STEPS if with_ctx else 0)
    width = SW_HEADS * HEAD_DIM
    n_out = N_TOK if with_ctx else N_LAT

    def q_idx(b, j):
        return (_attn_row_block(b, j, SW_STEPS, SW_TQ), 0)

    return pl.pallas_call(
        _swa_kernel,
        grid=(BATCH, steps),
        in_specs=[
            pl.BlockSpec(memory_space=pltpu.SMEM),
            pl.BlockSpec((SW_TQ, width), q_idx),
            pl.BlockSpec((SEQ, LANES), lambda b, j: (b, 0)),
            pl.BlockSpec((SEQ, LANES), lambda b, j: (b, 0)),
            pl.BlockSpec((CTX_LEN, LANES), lambda b, j: (N_LAT // CTX_LEN + b, 0)),
            pl.BlockSpec((CTX_LEN, LANES), lambda b, j: (N_LAT // CTX_LEN + b, 0)),
        ],
        out_specs=pl.BlockSpec((SW_TQ, width), q_idx),
        out_shape=jax.ShapeDtypeStruct((n_out, width), BF16),
        compiler_params=_cparams(2),
        name="swa_attn",
    )(sink, qb, kb, vb, kb, vb)


HALO = 16


def _merge_kernel(x_ref, a_ref, b_ref, u_ref, up_ref, un_ref, cgb_ref, gate_ref, g1_ref, cw_ref,
                  wa_ref, wb_ref, wc_ref, wo_ref, o_ref):
    i = pl.program_id(0)
    local = lax.broadcasted_iota(jnp.int32, (TM, 1), 0)
    row = i * TM + local
    is_lat = row < N_LAT
    pos = jnp.where(is_lat, row & (SEQ - 1), (row - N_LAT) & (CTX_LEN - 1))
    seq_first = pos == 0
    seq_last = pos == jnp.where(is_lat, SEQ - 1, CTX_LEN - 1)

    u = u_ref[...].astype(F32)
    prev_row = up_ref[...].astype(F32)[HALO - 1:HALO, :]
    next_row = un_ref[...].astype(F32)[0:1, :]
    u_prev = jnp.where(local == 0, prev_row, pltpu.roll(u, 1, 0))
    u_prev = jnp.where(seq_first, 0.0, u_prev)
    u_next = jnp.where(local == TM - 1, next_row, pltpu.roll(u, TM - 1, 0))
    u_next = jnp.where(seq_last, 0.0, u_next)
    cw = cw_ref[...]
    y = cw[0:1, :] * u_prev + cw[1:2, :] * u + cw[2:3, :] * u_next
    c_out = (cgb_ref[...].astype(F32) * y).astype(BF16)

    merged = gate_ref[:, 0:D_MODEL].astype(F32) * _dot(a_ref[...], wa_ref[...])
    merged = merged + gate_ref[:, D_MODEL:2 * D_MODEL].astype(F32) * _dot(b_ref[...], wb_ref[...])
    merged = merged + gate_ref[:, 2 * D_MODEL:3 * D_MODEL].astype(F32) * _dot(c_out, wc_ref[...])
    o_ref[...] = x_ref[...] + g1_ref[...] * _dot(merged.astype(BF16), wo_ref[...])


def _merge(xf, a, b, u, cgb, gates, mods, conv_w, wa, wb, wc, wo, n_rows):
    halo_blocks = N_TOK // HALO
    per = TM // HALO

    def tok(wd):
        return pl.BlockSpec((TM, wd), lambda i: (i, 0))

    return pl.pallas_call(
        _merge_kernel,
        grid=(n_rows // TM,),
        in_specs=[
            tok(D_MODEL), tok(512), tok(512), tok(CONV_WIDTH),
            pl.BlockSpec((HALO, CONV_WIDTH), lambda i: (jnp.maximum(i * per - 1, 0), 0)),
            pl.BlockSpec((HALO, CONV_WIDTH), lambda i: (jnp.minimum((i + 1) * per, halo_blocks - 1), 0)),
            tok(CONV_WIDTH), tok(3 * D_MODEL),
            _mod_spec(2),
            _const_spec((3, CONV_WIDTH)),
            _const_spec((512, D_MODEL)), _const_spec((512, D_MODEL)), _const_spec((512, D_MODEL)),
            _const_spec((D_MODEL, D_MODEL)),
        ],
        out_specs=tok(D_MODEL),
        out_shape=jax.ShapeDtypeStruct((n_rows, D_MODEL), F32),
        compiler_params=_cparams(1),
        name="merge",
    )(xf, a, b, u, u, u, cgb, gates, mods, conv_w, wa, wb, wc, wo)


FF_CHUNKS = tuple((c, min(c + 512, D_FF)) for c in range(0, D_FF, 512))


def _ffn_kernel(x_ref, sh_ref, sc_ref, g2_ref, ng_ref, fg_ref, wu_ref, wd_ref, o_ref, *, final):
    x = x_ref[...]
    h = _mod_norm(x, ng_ref[...], sh_ref[...], sc_ref[...]).astype(BF16)
    acc = None
    for c0, c1 in FF_CHUNKS:
        gate = _dot(h, wu_ref[:, c0:c1])
        up = _dot(h, wu_ref[:, D_FF + c0:D_FF + c1])
        act = (gate * jax.nn.sigmoid(gate) * up).astype(BF16)
        part = _dot(act, wd_ref[c0:c1, :])
        acc = part if acc is None else acc + part
    xn = x + g2_ref[...] * acc
    if final:
        xn = xn * lax.rsqrt(jnp.mean(xn * xn, axis=-1, keepdims=True) + NORM_EPS) * fg_ref[...]
    o_ref[...] = xn


def _ffn(xf, mods, ng, fg, wu, wd, final):
    n_rows = xf.shape[0]
    return pl.pallas_call(
        functools.partial(_ffn_kernel, final=final),
        grid=(n_rows // TM,),
        in_specs=[
            pl.BlockSpec((TM, D_MODEL), lambda i: (i, 0)),
            _mod_spec(3), _mod_spec(4), _mod_spec(5),
            _const_spec((1, D_MODEL)), _const_spec((1, D_MODEL)),
            _const_spec((D_MODEL, 2 * D_FF)), _const_spec((D_FF, D_MODEL)),
        ],
        out_specs=pl.BlockSpec((TM, D_MODEL), lambda i: (i, 0)),
        out_shape=jax.ShapeDtypeStruct((n_rows, D_MODEL), F32),
        compiler_params=_cparams(1),
        name="ffn",
    )(xf, mods, mods, mods, ng, fg, wu, wd)


def _in_perm():
    perm = np.arange(N_IN)
    half = HEAD_DIM // 2
    for p in range(SW_HEADS // 2):
        for lane in range(LANES):
            quarter, f = divmod(lane, half)
            head = p + (SW_HEADS // 2) * (quarter % 2)
            perm[C_QB + LANES * p + lane] = 1536 + head * HEAD_DIM + f + half * (quarter // 2)
    for lane in range(LANES):
        quarter, f = divmod(lane, half)
        perm[C_KB + lane] = 2048 + (quarter % 2) * HEAD_DIM + f + half * (quarter // 2)
    return perm


def _swa_out_perm():
    perm = np.zeros(SW_HEADS * HEAD_DIM, np.int64)
    for col in range(SW_HEADS * HEAD_DIM):
        p, r = divmod(col, LANES)
        g, d = divmod(r, HEAD_DIM)
        perm[col] = (p + (SW_HEADS // 2) * g) * HEAD_DIM + d
    return perm


def _rope_tables():
    t = np.arange(SEQ)
    row = (t // GRID_W).astype(np.float32)
    col = (t % GRID_W).astype(np.float32)
    inv = jnp.asarray(ROPE_THETA, F32) ** (-jnp.arange(ROPE_FREQS, dtype=F32) / ROPE_FREQS)
    ang = jnp.concatenate([jnp.asarray(row)[:, None] * inv, jnp.asarray(col)[:, None] * inv], axis=-1)
    cos = jnp.cos(ang)
    sin = jnp.sin(ang)
    cos4 = jnp.concatenate([cos, cos, cos, cos], axis=-1)
    sin4 = jnp.concatenate([-sin, -sin, sin, sin], axis=-1)
    cos4 = jnp.concatenate([cos4, jnp.ones((TM, LANES), F32)], axis=0)
    sin4 = jnp.concatenate([sin4, jnp.zeros((TM, LANES), F32)], axis=0)
    return cos4, sin4


def _bias_tiles(rpb):
    qc = np.arange(GRID_W)[:, None]
    kc = np.arange(GRID_W)[None, :]
    start = np.clip(qc - NA_WIN_COLS // 2, 0, GRID_W - NA_WIN_COLS)
    valid = (kc >= start) & (kc < start + NA_WIN_COLS)
    dcol = np.clip(kc - qc + NA_WIN_COLS - 1, 0, 2 * NA_WIN_COLS - 2)
    tiles = rpb.astype(F32)[:, :, dcol]
    tiles = jnp.where(jnp.asarray(valid)[None, None], tiles, NEG)
    return tiles.transpose(1, 0, 2, 3)


def kernel(x, c, ctx, c_ctx, w_ada, b_ada, norm1_g, norm2_g, w_in, b_gate, na_rpb, sw_sink, conv_w,
           w_br_a, w_br_b, w_br_c, w_o, w_ffn_up, w_ffn_down, final_norm_g):
    xf = jnp.concatenate([x.reshape(N_LAT, D_MODEL), ctx.reshape(N_CTX, D_MODEL)], axis=0)
    cvecs = jnp.concatenate([c, c_ctx[None, :], jnp.zeros((8 - BATCH - 1, D_MODEL), F32)], axis=0)
    mods_all = _ada(cvecs, w_ada, b_ada).reshape(DEPTH, 8, 1, 6 * D_MODEL)

    in_perm = _in_perm()
    col_scale = np.ones((N_IN,), np.float32)
    col_scale[C_QA:C_KA] = HEAD_DIM ** -0.5
    col_scale[C_QB:C_KB] = HEAD_DIM ** -0.5
    b_perm = _swa_out_perm()
    rope_c, rope_s = _rope_tables()
    final_g = final_norm_g.reshape(1, D_MODEL)

    for l in range(DEPTH):
        last = l == DEPTH - 1
        mods = mods_all[l]
        w_in_l = (w_in[l][:, in_perm] * col_scale).astype(BF16)
        qa, ka, va, qb, kb, vb, u, cgb, gates = _inproj(
            xf, mods, norm1_g[l].reshape(1, D_MODEL), w_in_l, b_gate[l].reshape(1, 3 * D_MODEL),
            rope_c, rope_s)
        a_out = _na_attention(qa, ka, va, _bias_tiles(na_rpb[l]), not last)
        b_out = _swa_attention(sw_sink[l], qb, kb, vb, not last)
        n_rows = N_LAT if last else N_TOK
        xf = _merge(xf, a_out, b_out, u, cgb, gates, mods, conv_w[l],
                    w_br_a[l].astype(BF16), w_br_b[l][b_perm, :].astype(BF16), w_br_c[l].astype(BF16),
                    w_o[l].astype(BF16), n_rows)
        xf = _ffn(xf, mods, norm2_g[l].reshape(1, D_MODEL), final_g,
                  w_ffn_up[l].astype(BF16), w_ffn_down[l].astype(BF16), last)
    return xf.reshape(BATCH, SEQ, D_MODEL)
```

```python
import functools

import numpy as np
import jax
import jax.numpy as jnp
from jax import lax
from jax.experimental import pallas as pl
from jax.experimental.pallas import tpu as pltpu

F32 = jnp.float32
BF16 = jnp.bfloat16

D_MODEL = 1024
BATCH = 4
SEQ = 8192
DEPTH = 4
GRID_W = 64
GRID_ROWS = SEQ // GRID_W
CTX_LEN = 256
HEAD_DIM = 64
NORM_EPS = 1e-6
NA_HEADS = 8
NA_WIN_ROWS = 8
NA_WIN_COLS = 16
SW_HEADS = 8
SW_WINDOW = 128
ROPE_THETA = 10000.0
ROPE_FREQS = HEAD_DIM // 4
CONV_WIDTH = 512
D_FF = 2816
N_IN = 6912

N_LAT = BATCH * SEQ
N_CTX = BATCH * CTX_LEN
N_TOK = N_LAT + N_CTX

LANES = 128
NEG = -1e30

C_QA, C_KA, C_VA, C_QB, C_KB, C_VB, C_CX, C_CGB, C_CGC, C_ZG = (
    0, 512, 1024, 1536, 2048, 2176, 2304, 2816, 3328, 3840)

TM = 512
NA_QROWS = 4
NA_KROWS = NA_QROWS + NA_WIN_ROWS
NA_TQ = NA_QROWS * GRID_W
NA_TK = NA_KROWS * GRID_W
NA_STEPS = GRID_ROWS // NA_QROWS
NA_CTX_STEPS = CTX_LEN // NA_TQ
SW_TQ = 128
SW_TK = 3 * SW_TQ
SW_STEPS = SEQ // SW_TQ
SW_CTX_STEPS = CTX_LEN // SW_TQ
SW_GROUP = 4
VMEM_LIMIT = 56 * 1024 * 1024


def _cparams(n_axes):
    return pltpu.CompilerParams(dimension_semantics=("arbitrary",) * n_axes,
                                vmem_limit_bytes=VMEM_LIMIT)


def _const_spec(shape):
    nd = len(shape)
    return pl.BlockSpec(shape, lambda *_: (0,) * nd, pipeline_mode=pl.Buffered(1))


def _dot(a, b):
    return jnp.dot(a, b, preferred_element_type=F32)


def _dot_nt(a, b):
    return lax.dot_general(a, b, (((1,), (1,)), ((), ())), preferred_element_type=F32)


def _mod_norm(x, g, shift, scale):
    y = x * lax.rsqrt(jnp.mean(x * x, axis=-1, keepdims=True) + NORM_EPS) * g
    return y * (1.0 + scale) + shift


def _ada_kernel(c_ref, w_ref, b_ref, o_ref):
    c = c_ref[...]
    sc = c * jax.nn.sigmoid(c)
    o_ref[...] = jnp.dot(sc, w_ref[...], preferred_element_type=F32,
                         precision=lax.Precision.HIGHEST) + b_ref[...]


def _ada(cvecs, w_ada, b_ada):
    tn = 1536
    return pl.pallas_call(
        _ada_kernel,
        grid=(DEPTH, 6 * D_MODEL // tn),
        in_specs=[
            pl.BlockSpec((8, D_MODEL), lambda l, n: (0, 0)),
            pl.BlockSpec((None, D_MODEL, tn), lambda l, n: (l, 0, n)),
            pl.BlockSpec((None, 1, tn), lambda l, n: (l, 0, n)),
        ],
        out_specs=pl.BlockSpec((None, 8, tn), lambda l, n: (l, 0, n)),
        out_shape=jax.ShapeDtypeStruct((DEPTH, 8, 6 * D_MODEL), F32),
        compiler_params=_cparams(2),
        name="ada",
    )(cvecs, w_ada, b_ada.reshape(DEPTH, 1, 6 * D_MODEL))


def _mod_row(i):
    lat_blocks = N_LAT // TM
    return jnp.where(i < lat_blocks, i // (SEQ // TM), BATCH)


def _mod_spec(part):
    return pl.BlockSpec((None, 1, D_MODEL), lambda i: (_mod_row(i), 0, part))


def _inproj_kernel(x_ref, sh_ref, sc_ref, g_ref, w_ref, bg_ref, cos_ref, sin_ref,
                   qa_ref, ka_ref, va_ref, qb_ref, kb_ref, vb_ref, u_ref, cgb_ref, gate_ref):
    h = _mod_norm(x_ref[...], g_ref[...], sh_ref[...], sc_ref[...]).astype(BF16)

    def proj(c0, c1):
        return _dot(h, w_ref[:, c0:c1])

    qa_ref[...] = proj(C_QA, C_KA).astype(BF16)
    ka_ref[...] = proj(C_KA, C_VA).astype(BF16)
    va_ref[...] = proj(C_VA, C_QB).astype(BF16)

    cosv = cos_ref[...]
    sinv = sin_ref[...]

    def rope(t):
        return t * cosv + pltpu.roll(t, LANES // 2, 1) * sinv

    qb = proj(C_QB, C_KB)
    for p in range(4):
        qb_ref[:, LANES * p:LANES * (p + 1)] = rope(qb[:, LANES * p:LANES * (p + 1)]).astype(BF16)
    kv = proj(C_KB, C_CX)
    kb_ref[...] = rope(kv[:, :LANES]).astype(BF16)
    vb_ref[...] = kv[:, LANES:].astype(BF16)

    u_ref[...] = (proj(C_CGC, C_ZG) * proj(C_CX, C_CGB)).astype(BF16)
    cgb_ref[...] = proj(C_CGB, C_CGC).astype(BF16)
    for k in range(3):
        z = proj(C_ZG + D_MODEL * k, C_ZG + D_MODEL * (k + 1))
        gate_ref[:, D_MODEL * k:D_MODEL * (k + 1)] = jax.nn.sigmoid(
            z + bg_ref[:, D_MODEL * k:D_MODEL * (k + 1)]).astype(BF16)


def _inproj(xf, mods, g, w, bg, rope_c, rope_s):
    lat_blocks = N_LAT // TM
    seq_blocks = SEQ // TM

    def rope_idx(i):
        return (jnp.where(i < lat_blocks, i % seq_blocks, seq_blocks), 0)

    widths = (512, 512, 512, 512, 128, 128, 512, 512, 3072)
    return pl.pallas_call(
        _inproj_kernel,
        grid=(N_TOK // TM,),
        in_specs=[
            pl.BlockSpec((TM, D_MODEL), lambda i: (i, 0)),
            _mod_spec(0), _mod_spec(1),
            _const_spec((1, D_MODEL)),
            _const_spec((D_MODEL, N_IN)),
            _const_spec((1, 3 * D_MODEL)),
            pl.BlockSpec((TM, LANES), rope_idx),
            pl.BlockSpec((TM, LANES), rope_idx),
        ],
        out_specs=[pl.BlockSpec((TM, wd), lambda i: (i, 0)) for wd in widths],
        out_shape=[jax.ShapeDtypeStruct((N_TOK, wd), BF16) for wd in widths],
        compiler_params=_cparams(1),
        name="inproj",
    )(xf, mods, mods, g, w, bg, rope_c, rope_s)


def _attend(qs, k_all, v_all, bias, sink=None):
    s = _dot_nt(qs, k_all)
    if bias is not None:
        nb = bias.shape[1]
        s = jnp.concatenate([s[:, :nb] + bias, s[:, nb:]], axis=1)
    m = jnp.max(s, axis=-1, keepdims=True)
    if sink is not None:
        m = jnp.maximum(m, sink)
    e = jnp.exp(s - m).astype(BF16)
    v_ext = jnp.concatenate([v_all, jnp.ones_like(v_all)], axis=1)
    o = _dot(e, v_ext)
    den = o[:, LANES:]
    if sink is not None:
        den = den + jnp.exp(sink - m)
    return o[:, :LANES] / den


def _lane_iota():
    return lax.broadcasted_iota(jnp.int32, (1, LANES), 1)


def _na_kernel(q_ref, k_ref, v_ref, kc_ref, vc_ref, bt_ref, o_ref, bias_scr):
    j = pl.program_id(1)
    low_half = _lane_iota() < HEAD_DIM

    def head_pair(p, kstart):
        sl = slice(LANES * p, LANES * (p + 1))
        qp = q_ref[:, sl]
        qs = jnp.concatenate([jnp.where(low_half, qp, jnp.zeros_like(qp)),
                              jnp.where(low_half, jnp.zeros_like(qp), qp)], axis=0)
        if kstart is None:
            k_all, v_all, bias = kc_ref[:, sl], vc_ref[:, sl], None
        else:
            k_all = jnp.concatenate([k_ref[pl.ds(kstart, NA_TK), sl], kc_ref[:, sl]], axis=0)
            v_all = jnp.concatenate([v_ref[pl.ds(kstart, NA_TK), sl], vc_ref[:, sl]], axis=0)
            bias = bias_scr[2 * p:2 * p + 2].reshape(2 * NA_TQ, NA_TK)
        o = _attend(qs, k_all, v_all, bias)
        o_ref[:, sl] = jnp.where(low_half, o[:NA_TQ], o[NA_TQ:]).astype(o_ref.dtype)

    def all_pairs(kstart):
        for p in range(NA_HEADS // 2):
            head_pair(p, kstart)

    @pl.when(j < NA_STEPS)
    def _latent():
        kr0 = jnp.clip(NA_QROWS * j - NA_WIN_ROWS // 2, 0, GRID_ROWS - NA_KROWS)

        @pl.when((j <= 1) | (j == NA_STEPS - 1))
        def _regen():
            for i in range(NA_QROWS):
                qr = NA_QROWS * j + i
                r0 = jnp.clip(qr - NA_WIN_ROWS // 2, 0, GRID_ROWS - NA_WIN_ROWS)
                for jj in range(0, NA_KROWS, 2):
                    tiles = []
                    for kk in (jj, jj + 1):
                        kr = kr0 + kk
                        valid = (kr >= r0) & (kr < r0 + NA_WIN_ROWS)
                        idx = jnp.clip(kr - qr + NA_WIN_ROWS - 1, 0, 2 * NA_WIN_ROWS - 2)
                        tiles.append(jnp.where(valid, bt_ref[idx], NEG))
                    bias_scr[:, GRID_W * i:GRID_W * (i + 1), GRID_W * jj:GRID_W * (jj + 2)] = (
                        jnp.concatenate(tiles, axis=-1))

        all_pairs(pl.multiple_of(kr0 * GRID_W, GRID_W))

    @pl.when(j >= NA_STEPS)
    def _context():
        all_pairs(None)


def _attn_row_block(b, j, lat_steps, tq):
    lat = b * lat_steps + j
    ctx = N_LAT // tq + b * (CTX_LEN // tq) + (j - lat_steps)
    return jnp.where(j < lat_steps, lat, ctx)


def _batch_spec(rows, width):
    return pl.BlockSpec((rows, width), lambda b, j: (b, 0), pipeline_mode=pl.Buffered(1))


def _ctx_spec(width):
    return pl.BlockSpec((CTX_LEN, width), lambda b, j: (N_LAT // CTX_LEN + b, 0))


def _na_attention(qa, ka, va, btab, with_ctx):
    steps = NA_STEPS + (NA_CTX_STEPS if with_ctx else 0)
    width = NA_HEADS * HEAD_DIM
    n_out = N_TOK if with_ctx else N_LAT

    def q_idx(b, j):
        return (_attn_row_block(b, j, NA_STEPS, NA_TQ), 0)

    return pl.pallas_call(
        _na_kernel,
        grid=(BATCH, steps),
        in_specs=[
            pl.BlockSpec((NA_TQ, width), q_idx),
            _batch_spec(SEQ, width), _batch_spec(SEQ, width),
            _ctx_spec(width), _ctx_spec(width),
            _const_spec((2 * NA_WIN_ROWS - 1, NA_HEADS, GRID_W, GRID_W)),
        ],
        out_specs=pl.BlockSpec((NA_TQ, width), q_idx),
        out_shape=jax.ShapeDtypeStruct((n_out, width), BF16),
        scratch_shapes=[pltpu.VMEM((NA_HEADS, NA_TQ, NA_TK), F32)],
        compiler_params=_cparams(2),
        name="na_attn",
    )(qa, ka, va, ka, va, btab)


def _swa_kernel(sink_ref, q_ref, k_ref, v_ref, kc_ref, vc_ref, o_ref, mask_scr):
    j = pl.program_id(1)
    lane = _lane_iota()
    low_half = lane < HEAD_DIM
    group0 = (lane % HEAD_DIM) < HEAD_DIM // 2

    def kv_group(g, ks):
        in_group = group0 if g == 0 else ~group0
        qs = jnp.concatenate(
            [jnp.where(in_group, q_ref[:, LANES * p:LANES * (p + 1)], jnp.zeros((SW_TQ, LANES), BF16))
             for p in range(SW_GROUP)], axis=0)
        sink = jnp.concatenate(
            [jnp.full((SW_TQ, 1), sink_ref[p + SW_GROUP * g], F32) for p in range(SW_GROUP)], axis=0)
        if ks is None:
            k_all, v_all, bias = kc_ref[...], vc_ref[...], None
        else:
            k_all = jnp.concatenate([k_ref[pl.ds(ks, SW_TK), :], kc_ref[...]], axis=0)
            v_all = jnp.concatenate([v_ref[pl.ds(ks, SW_TK), :], vc_ref[...]], axis=0)
            bias = mask_scr[...]
        return _attend(qs, k_all, v_all, bias, sink=sink)

    def all_heads(ks):
        o0 = kv_group(0, ks)
        o1 = kv_group(1, ks)
        for p in range(SW_GROUP):
            rows = slice(SW_TQ * p, SW_TQ * (p + 1))
            o_ref[:, LANES * p:LANES * (p + 1)] = jnp.where(low_half, o0[rows], o1[rows]).astype(o_ref.dtype)

    @pl.when(j < SW_STEPS)
    def _latent():
        q0 = j * SW_TQ
        ks = pl.multiple_of(jnp.clip(q0 - SW_TQ, 0, SEQ - SW_TK), SW_TQ)

        @pl.when((j <= 1) | (j == SW_STEPS - 1))
        def _regen():
            rel = ((ks - q0) + lax.broadcasted_iota(jnp.int32, (SW_TQ, SW_TK), 1)
                   - lax.broadcasted_iota(jnp.int32, (SW_TQ, SW_TK), 0))
            band = jnp.where(jnp.abs(rel) <= SW_WINDOW, 0.0, NEG)
            for p in range(SW_GROUP):
                mask_scr[SW_TQ * p:SW_TQ * (p + 1), :] = band

        all_heads(ks)

    @pl.when(j >= SW_STEPS)
    def _context():
        all_heads(None)


def _swa_attention(sink, qb, kb, vb, with_ctx):
    steps = SW_STEPS + (SW_CTX_STEPS if with_ctx else 0)
    width = SW_HEADS * HEAD_DIM
    n_out = N_TOK if with_ctx else N_LAT

    def q_idx(b, j):
        return (_attn_row_block(b, j, SW_STEPS, SW_TQ), 0)

    return pl.pallas_call(
        _swa_kernel,
        grid=(BATCH, steps),
        in_specs=[
            pl.BlockSpec(memory_space=pltpu.SMEM),
            pl.BlockSpec((SW_TQ, width), q_idx),
            _batch_spec(SEQ, LANES), _batch_spec(SEQ, LANES),
            _ctx_spec(LANES), _ctx_spec(LANES),
        ],
        out_specs=pl.BlockSpec((SW_TQ, width), q_idx),
        out_shape=jax.ShapeDtypeStruct((n_out, width), BF16),
        scratch_shapes=[pltpu.VMEM((SW_GROUP * SW_TQ, SW_TK), F32)],
        compiler_params=_cparams(2),
        name="swa_attn",
    )(sink, qb, kb, vb, kb, vb)


HALO = 16


def _merge_kernel(x_ref, a_ref, b_ref, u_ref, up_ref, un_ref, cgb_ref, gate_ref, g1_ref, cw_ref,
                  wa_ref, wb_ref, wc_ref, wo_ref, o_ref):
    i = pl.program_id(0)
    local = lax.broadcasted_iota(jnp.int32, (TM, 1), 0)
    row = i * TM + local
    is_lat = row < N_LAT
    pos = jnp.where(is_lat, row & (SEQ - 1), (row - N_LAT) & (CTX_LEN - 1))
    seq_first = pos == 0
    seq_last = pos == jnp.where(is_lat, SEQ - 1, CTX_LEN - 1)

    u = u_ref[...].astype(F32)
    prev_row = up_ref[...].astype(F32)[HALO - 1:HALO, :]
    next_row = un_ref[...].astype(F32)[0:1, :]
    u_prev = jnp.where(local == 0, prev_row, pltpu.roll(u, 1, 0))
    u_prev = jnp.where(seq_first, 0.0, u_prev)
    u_next = jnp.where(local == TM - 1, next_row, pltpu.roll(u, TM - 1, 0))
    u_next = jnp.where(seq_last, 0.0, u_next)
    cw = cw_ref[...]
    y = cw[0:1, :] * u_prev + cw[1:2, :] * u + cw[2:3, :] * u_next
    c_out = (cgb_ref[...].astype(F32) * y).astype(BF16)

    merged = gate_ref[:, 0:D_MODEL].astype(F32) * _dot(a_ref[...], wa_ref[...])
    merged = merged + gate_ref[:, D_MODEL:2 * D_MODEL].astype(F32) * _dot(b_ref[...], wb_ref[...])
    merged = merged + gate_ref[:, 2 * D_MODEL:3 * D_MODEL].astype(F32) * _dot(c_out, wc_ref[...])
    o_ref[...] = x_ref[...] + g1_ref[...] * _dot(merged.astype(BF16), wo_ref[...])


def _merge(xf, a, b, u, cgb, gates, mods, conv_w, wa, wb, wc, wo, n_rows):
    halo_blocks = N_TOK // HALO
    per = TM // HALO

    def tok(wd):
        return pl.BlockSpec((TM, wd), lambda i: (i, 0))

    return pl.pallas_call(
        _merge_kernel,
        grid=(n_rows // TM,),
        in_specs=[
            tok(D_MODEL), tok(512), tok(512), tok(CONV_WIDTH),
            pl.BlockSpec((HALO, CONV_WIDTH), lambda i: (jnp.maximum(i * per - 1, 0), 0)),
            pl.BlockSpec((HALO, CONV_WIDTH), lambda i: (jnp.minimum((i + 1) * per, halo_blocks - 1), 0)),
            tok(CONV_WIDTH), tok(3 * D_MODEL),
            _mod_spec(2),
            _const_spec((3, CONV_WIDTH)),
            _const_spec((512, D_MODEL)), _const_spec((512, D_MODEL)), _const_spec((512, D_MODEL)),
            _const_spec((D_MODEL, D_MODEL)),
        ],
        out_specs=tok(D_MODEL),
        out_shape=jax.ShapeDtypeStruct((n_rows, D_MODEL), F32),
        compiler_params=_cparams(1),
        name="merge",
    )(xf, a, b, u, u, u, cgb, gates, mods, conv_w, wa, wb, wc, wo)


FF_CHUNKS = tuple((c, min(c + 512, D_FF)) for c in range(0, D_FF, 512))


def _ffn_kernel(x_ref, sh_ref, sc_ref, g2_ref, ng_ref, fg_ref, wu_ref, wd_ref, o_ref, *, final):
    x = x_ref[...]
    h = _mod_norm(x, ng_ref[...], sh_ref[...], sc_ref[...]).astype(BF16)
    acc = None
    for c0, c1 in FF_CHUNKS:
        gate = _dot(h, wu_ref[:, c0:c1])
        up = _dot(h, wu_ref[:, D_FF + c0:D_FF + c1])
        act = (gate * jax.nn.sigmoid(gate) * up).astype(BF16)
        part = _dot(act, wd_ref[c0:c1, :])
        acc = part if acc is None else acc + part
    xn = x + g2_ref[...] * acc
    if final:
        xn = xn * lax.rsqrt(jnp.mean(xn * xn, axis=-1, keepdims=True) + NORM_EPS) * fg_ref[...]
    o_ref[...] = xn


def _ffn(xf, mods, ng, fg, wu, wd, final):
    n_rows = xf.shape[0]
    return pl.pallas_call(
        functools.partial(_ffn_kernel, final=final),
        grid=(n_rows // TM,),
        in_specs=[
            pl.BlockSpec((TM, D_MODEL), lambda i: (i, 0)),
            _mod_spec(3), _mod_spec(4), _mod_spec(5),
            _const_spec((1, D_MODEL)), _const_spec((1, D_MODEL)),
            _const_spec((D_MODEL, 2 * D_FF)), _const_spec((D_FF, D_MODEL)),
        ],
        out_specs=pl.BlockSpec((TM, D_MODEL), lambda i: (i, 0)),
        out_shape=jax.ShapeDtypeStruct((n_rows, D_MODEL), F32),
        compiler_params=_cparams(1),
        name="ffn",
    )(xf, mods, mods, mods, ng, fg, wu, wd)


def _prep_w_in(w_in):
    scale = HEAD_DIM ** -0.5
    half = HEAD_DIM // 2
    lead = w_in.shape[:-1]
    qa = w_in[..., C_QA:C_KA] * scale
    qb = (w_in[..., C_QB:C_KB] * scale).reshape(*lead, 2, SW_GROUP, 2, half)
    qb = jnp.moveaxis(qb, -4, -2).reshape(*lead, SW_HEADS * HEAD_DIM)
    kb = w_in[..., C_KB:C_VB].reshape(*lead, 2, 2, half)
    kb = jnp.swapaxes(kb, -3, -2).reshape(*lead, 2 * HEAD_DIM)
    return jnp.concatenate([qa, w_in[..., C_KA:C_QB], qb, kb, w_in[..., C_VB:]], axis=-1).astype(BF16)


def _prep_w_br_b(w_br_b):
    w = w_br_b.reshape(DEPTH, 2, SW_GROUP, HEAD_DIM, D_MODEL)
    return jnp.swapaxes(w, 1, 2).reshape(DEPTH, SW_HEADS * HEAD_DIM, D_MODEL).astype(BF16)


def _rope_tables():
    t = np.arange(SEQ)
    row = (t // GRID_W).astype(np.float32)
    col = (t % GRID_W).astype(np.float32)
    inv = jnp.asarray(ROPE_THETA, F32) ** (-jnp.arange(ROPE_FREQS, dtype=F32) / ROPE_FREQS)
    ang = jnp.concatenate([jnp.asarray(row)[:, None] * inv, jnp.asarray(col)[:, None] * inv], axis=-1)
    cos = jnp.cos(ang)
    sin = jnp.sin(ang)
    cos4 = jnp.concatenate([cos, cos, cos, cos], axis=-1)
    sin4 = jnp.concatenate([-sin, -sin, sin, sin], axis=-1)
    cos4 = jnp.concatenate([cos4, jnp.ones((TM, LANES), F32)], axis=0)
    sin4 = jnp.concatenate([sin4, jnp.zeros((TM, LANES), F32)], axis=0)
    return cos4, sin4


def _bias_tiles(rpb):
    qc = np.arange(GRID_W)[:, None]
    kc = np.arange(GRID_W)[None, :]
    start = np.clip(qc - NA_WIN_COLS // 2, 0, GRID_W - NA_WIN_COLS)
    valid = (kc >= start) & (kc < start + NA_WIN_COLS)
    dcol = np.clip(kc - qc + NA_WIN_COLS - 1, 0, 2 * NA_WIN_COLS - 2)
    tiles = rpb.astype(F32)[..., dcol]
    tiles = jnp.where(jnp.asarray(valid), tiles, NEG)
    return jnp.swapaxes(tiles, 1, 2)


def kernel(x, c, ctx, c_ctx, w_ada, b_ada, norm1_g, norm2_g, w_in, b_gate, na_rpb, sw_sink, conv_w,
           w_br_a, w_br_b, w_br_c, w_o, w_ffn_up, w_ffn_down, final_norm_g):
    xf = jnp.concatenate([x.reshape(N_LAT, D_MODEL), ctx.reshape(N_CTX, D_MODEL)], axis=0)
    cvecs = jnp.concatenate([c, c_ctx[None, :], jnp.zeros((8 - BATCH - 1, D_MODEL), F32)], axis=0)
    mods_all = _ada(cvecs, w_ada, b_ada).reshape(DEPTH, 8, 1, 6 * D_MODEL)

    rope_c, rope_s = _rope_tables()
    final_g = final_norm_g.reshape(1, D_MODEL)
    w_in_b = _prep_w_in(w_in)
    w_a_b, w_b_b, w_c_b = w_br_a.astype(BF16), _prep_w_br_b(w_br_b), w_br_c.astype(BF16)
    w_o_b, w_up_b, w_dn_b = w_o.astype(BF16), w_ffn_up.astype(BF16), w_ffn_down.astype(BF16)
    btab = _bias_tiles(na_rpb)

    for l in range(DEPTH):
        last = l == DEPTH - 1
        mods = mods_all[l]
        qa, ka, va, qb, kb, vb, u, cgb, gates = _inproj(
            xf, mods, norm1_g[l].reshape(1, D_MODEL), w_in_b[l], b_gate[l].reshape(1, 3 * D_MODEL),
            rope_c, rope_s)
        a_out = _na_attention(qa, ka, va, btab[l], not last)
        b_out = _swa_attention(sw_sink[l], qb, kb, vb, not last)
        n_rows = N_LAT if last else N_TOK
        xf = _merge(xf, a_out, b_out, u, cgb, gates, mods, conv_w[l],
                    w_a_b[l], w_b_b[l], w_c_b[l], w_o_b[l], n_rows)
        xf = _ffn(xf, mods, norm2_g[l].reshape(1, D_MODEL), final_g, w_up_b[l], w_dn_b[l], last)
    return xf.reshape(BATCH, SEQ, D_MODEL)
```

```python
import functools

import numpy as np
import jax
import jax.numpy as jnp
from jax import lax
from jax.experimental import pallas as pl
from jax.experimental.pallas import tpu as pltpu

F32 = jnp.float32
BF16 = jnp.bfloat16

D_MODEL = 1024
BATCH = 4
SEQ = 8192
DEPTH = 4
GRID_W = 64
GRID_ROWS = SEQ // GRID_W
CTX_LEN = 256
HEAD_DIM = 64
NORM_EPS = 1e-6
NA_HEADS = 8
NA_WIN_ROWS = 8
NA_WIN_COLS = 16
SW_HEADS = 8
SW_WINDOW = 128
ROPE_THETA = 10000.0
ROPE_FREQS = HEAD_DIM // 4
CONV_WIDTH = 512
D_FF = 2816
N_IN = 6912
N_ROT = 768

N_LAT = BATCH * SEQ
N_CTX = BATCH * CTX_LEN
N_TOK = N_LAT + N_CTX

LANES = 128
NEG = -1e30

C_QA, C_KA, C_VA, C_QB, C_KB, C_VB, C_CX, C_CGB, C_CGC, C_ZG = (
    0, 512, 1024, 1536, 2048, 2176, 2304, 2816, 3328, 3840)

TM = 512
NA_QROWS = 2
NA_KROWS = NA_QROWS + NA_WIN_ROWS
NA_TQ = NA_QROWS * GRID_W
NA_TK = NA_KROWS * GRID_W
NA_STEPS = GRID_ROWS // NA_QROWS
NA_CTX_STEPS = CTX_LEN // NA_TQ
SW_TQ = 128
SW_TK = 3 * SW_TQ
SW_STEPS = SEQ // SW_TQ
SW_CTX_STEPS = CTX_LEN // SW_TQ
SW_GROUP = 4
VMEM_LIMIT = 56 * 1024 * 1024


def _cparams(n_axes):
    return pltpu.CompilerParams(dimension_semantics=("arbitrary",) * n_axes,
                                vmem_limit_bytes=VMEM_LIMIT)


def _const_spec(shape):
    nd = len(shape)
    return pl.BlockSpec(shape, lambda *_: (0,) * nd, pipeline_mode=pl.Buffered(1))


def _dot(a, b):
    return jnp.dot(a, b, preferred_element_type=F32)


def _dot_nt(a, b):
    return lax.dot_general(a, b, (((1,), (1,)), ((), ())), preferred_element_type=F32)


def _mod_norm(x, g, shift, scale):
    y = x * lax.rsqrt(jnp.mean(x * x, axis=-1, keepdims=True) + NORM_EPS) * g
    return y * (1.0 + scale) + shift


def _ada_kernel(c_ref, w_ref, b_ref, o_ref):
    c = c_ref[...]
    sc = c * jax.nn.sigmoid(c)
    o_ref[...] = jnp.dot(sc, w_ref[...], preferred_element_type=F32,
                         precision=lax.Precision.HIGHEST) + b_ref[...]


def _ada(cvecs, w_ada, b_ada):
    tn = 1536
    return pl.pallas_call(
        _ada_kernel,
        grid=(DEPTH, 6 * D_MODEL // tn),
        in_specs=[
            pl.BlockSpec((8, D_MODEL), lambda l, n: (0, 0)),
            pl.BlockSpec((None, D_MODEL, tn), lambda l, n: (l, 0, n)),
            pl.BlockSpec((None, 1, tn), lambda l, n: (l, 0, n)),
        ],
        out_specs=pl.BlockSpec((None, 8, tn), lambda l, n: (l, 0, n)),
        out_shape=jax.ShapeDtypeStruct((DEPTH, 8, 6 * D_MODEL), F32),
        compiler_params=_cparams(2),
        name="ada",
    )(cvecs, w_ada, b_ada.reshape(DEPTH, 1, 6 * D_MODEL))


def _mod_row(i):
    lat_blocks = N_LAT // TM
    return jnp.where(i < lat_blocks, i // (SEQ // TM), BATCH)


def _mod_spec(part):
    return pl.BlockSpec((None, 1, D_MODEL), lambda i: (_mod_row(i), 0, part))


def _load_tokens(x_ref, xc_ref):
    if xc_ref is None:
        return x_ref[...]
    return jnp.where(pl.program_id(0) < N_LAT // TM, x_ref[...], xc_ref[...])


def _inproj_kernel(*refs, split):
    x_ref, xc_ref = (refs[0], refs[1]) if split else (refs[0], None)
    (sh_ref, sc_ref, g_ref, w_ref, wr_ref, bg_ref, cos_ref, sin_ref,
     qa_ref, ka_ref, va_ref, qb_ref, kb_ref, vb_ref, u_ref, cgb_ref, gate_ref) = refs[2 if split else 1:]
    h = _mod_norm(_load_tokens(x_ref, xc_ref), g_ref[...], sh_ref[...], sc_ref[...]).astype(BF16)
    q_scale = HEAD_DIM ** -0.5

    def proj(c0, c1):
        return _dot(h, w_ref[:, c0:c1])

    qa_ref[...] = (proj(C_QA, C_KA) * q_scale).astype(BF16)
    ka_ref[...] = proj(C_KA, C_VA).astype(BF16)
    va_ref[...] = proj(C_VA, C_QB).astype(BF16)

    cosv = cos_ref[...]
    sinv = sin_ref[...]

    def rope(t):
        return t * cosv + pltpu.roll(t, LANES // 2, 1) * sinv

    rot = _dot(h, wr_ref[...])
    for p in range(4):
        qb_ref[:, LANES * p:LANES * (p + 1)] = (
            rope(rot[:, LANES * p:LANES * (p + 1)]) * q_scale).astype(BF16)
    kb_ref[...] = rope(rot[:, 4 * LANES:5 * LANES]).astype(BF16)
    vb_ref[...] = rot[:, 5 * LANES:].astype(BF16)

    u_ref[...] = (proj(C_CGC, C_ZG) * proj(C_CX, C_CGB)).astype(BF16)
    cgb_ref[...] = proj(C_CGB, C_CGC).astype(BF16)
    for k in range(3):
        z = proj(C_ZG + D_MODEL * k, C_ZG + D_MODEL * (k + 1))
        gate_ref[:, D_MODEL * k:D_MODEL * (k + 1)] = jax.nn.sigmoid(
            z + bg_ref[:, D_MODEL * k:D_MODEL * (k + 1)]).astype(BF16)


def _token_specs(split):
    lat_blocks = N_LAT // TM
    if not split:
        return [pl.BlockSpec((TM, D_MODEL), lambda i: (i, 0))]
    return [pl.BlockSpec((TM, D_MODEL), lambda i: (jnp.minimum(i, lat_blocks - 1), 0)),
            pl.BlockSpec((TM, D_MODEL), lambda i: (jnp.maximum(i - lat_blocks, 0), 0))]


def _inproj(xs, mods, g, w, w_rot, bg, rope_c, rope_s):
    lat_blocks = N_LAT // TM
    seq_blocks = SEQ // TM
    split = len(xs) == 2

    def rope_idx(i):
        return (jnp.where(i < lat_blocks, i % seq_blocks, seq_blocks), 0)

    widths = (512, 512, 512, 512, 128, 128, 512, 512, 3072)
    return pl.pallas_call(
        functools.partial(_inproj_kernel, split=split),
        grid=(N_TOK // TM,),
        in_specs=_token_specs(split) + [
            _mod_spec(0), _mod_spec(1),
            _const_spec((1, D_MODEL)),
            _const_spec((D_MODEL, N_IN)),
            _const_spec((D_MODEL, N_ROT)),
            _const_spec((1, 3 * D_MODEL)),
            pl.BlockSpec((TM, LANES), rope_idx),
            pl.BlockSpec((TM, LANES), rope_idx),
        ],
        out_specs=[pl.BlockSpec((TM, wd), lambda i: (i, 0)) for wd in widths],
        out_shape=[jax.ShapeDtypeStruct((N_TOK, wd), BF16) for wd in widths],
        compiler_params=_cparams(1),
        name="inproj",
    )(*xs, mods, mods, g, w, w_rot, bg, rope_c, rope_s)


ATTN_LAG = 3
ATTN_SLOTS = 3


def _lane_iota():
    return lax.broadcasted_iota(jnp.int32, (1, LANES), 1)


def _pipeline_step(n, stage_a, stage_b, stage_c, stage_d):
    for r in range(ATTN_SLOTS):
        @pl.when(n % ATTN_SLOTS == r)
        def _step(r=r):
            stage_d(r)
            stage_c((r + 1) % ATTN_SLOTS)
            stage_b((r + 2) % ATTN_SLOTS)
            stage_a(r)


def _init_stage_buffers(s_scr, m_scr, pv_scr):
    s_scr[...] = jnp.zeros_like(s_scr)
    m_scr[...] = jnp.zeros_like(m_scr)
    pv_scr[...] = jnp.ones_like(pv_scr)


def _scores(qs, k_all, bias):
    s = _dot_nt(qs, k_all)
    nb = bias.shape[1]
    return jnp.concatenate([s[:, :nb] + bias, s[:, nb:]], axis=1)


def _row_max(s, sink):
    m = jnp.max(s, axis=-1, keepdims=True)
    if sink is not None:
        m = jnp.maximum(m, sink)
    return jnp.broadcast_to(m, (s.shape[0], LANES))


def _exp_pv(s_ref, m, v_all):
    nk = v_all.shape[0]
    e = jnp.concatenate(
        [jnp.exp(s_ref[:, LANES * c:LANES * (c + 1)] - m) for c in range(nk // LANES)], axis=1).astype(BF16)
    return _dot(e, jnp.concatenate([v_all, jnp.ones_like(v_all)], axis=1))


def _attn_row_block(b, j, lat_steps, tq):
    lat = b * lat_steps + j
    ctx = N_LAT // tq + b * (CTX_LEN // tq) + (j - lat_steps)
    return jnp.where(j < lat_steps, lat, ctx)


def _batch_spec(rows, width):
    return pl.BlockSpec((rows, width), lambda b, n: (b, 0), pipeline_mode=pl.Buffered(1))


def _ctx_spec(width):
    return pl.BlockSpec((CTX_LEN, width), lambda b, n: (N_LAT // CTX_LEN + b, 0))


def _attn_call(kernel_fn, name, operands, lead_specs, kv_width, tq, lat_steps, n_blocks, scratch):
    width = NA_HEADS * HEAD_DIM
    n_out = N_LAT + (n_blocks - lat_steps) * tq * BATCH

    def q_idx(b, n):
        return (_attn_row_block(b, jnp.minimum(n, n_blocks - 1), lat_steps, tq), 0)

    def o_idx(b, n):
        return (_attn_row_block(b, jnp.maximum(n - ATTN_LAG, 0), lat_steps, tq), 0)

    return pl.pallas_call(
        functools.partial(kernel_fn, n_blocks=n_blocks),
        grid=(BATCH, n_blocks + ATTN_LAG),
        in_specs=lead_specs + [
            pl.BlockSpec((tq, width), q_idx),
            _batch_spec(SEQ, kv_width), _batch_spec(SEQ, kv_width),
            _ctx_spec(kv_width), _ctx_spec(kv_width),
        ],
        out_specs=pl.BlockSpec((tq, width), o_idx),
        out_shape=jax.ShapeDtypeStruct((n_out, width), BF16),
        scratch_shapes=scratch,
        compiler_params=_cparams(2),
        name=name,
    )(*operands)


NA_PAIRS = NA_HEADS // 2
NA_NK = NA_TK + CTX_LEN


def _na_kernel(bt_ref, q_ref, k_ref, v_ref, kc_ref, vc_ref, o_ref,
               bias_scr, s_scr, m_scr, pv_scr, *, n_blocks):
    n = pl.program_id(1)
    low_half = _lane_iota() < HEAD_DIM

    def window_row(j):
        return jnp.where(j < NA_STEPS,
                         jnp.clip(NA_QROWS * j - NA_WIN_ROWS // 2, 0, GRID_ROWS - NA_KROWS), 0)

    def window_start(chain):
        return pl.multiple_of(window_row(jnp.clip(chain, 0, n_blocks - 1)) * GRID_W, GRID_W)

    @pl.when(n == 0)
    def _init():
        _init_stage_buffers(s_scr, m_scr, pv_scr)

    jb = jnp.minimum(n, n_blocks - 1)

    @pl.when((jb <= 2) | (jb >= NA_STEPS - 2))
    def _regen():
        kr0 = window_row(jb)
        for i in range(NA_QROWS):
            qr = NA_QROWS * jb + i
            r0 = jnp.clip(qr - NA_WIN_ROWS // 2, 0, GRID_ROWS - NA_WIN_ROWS)
            for jj in range(0, NA_KROWS, 2):
                tiles = []
                for kk in (jj, jj + 1):
                    kr = kr0 + kk
                    valid = (kr >= r0) & (kr < r0 + NA_WIN_ROWS) & (jb < NA_STEPS)
                    idx = jnp.clip(kr - qr + NA_WIN_ROWS - 1, 0, 2 * NA_WIN_ROWS - 2)
                    tiles.append(jnp.where(valid, bt_ref[idx], NEG))
                bias_scr[:, GRID_W * i:GRID_W * (i + 1), GRID_W * jj:GRID_W * (jj + 2)] = (
                    jnp.concatenate(tiles, axis=-1))

    def lanes(p):
        return slice(LANES * p, LANES * (p + 1))

    def stage_a(slot):
        ks = window_start(n)
        for p in range(NA_PAIRS):
            qp = q_ref[:, lanes(p)]
            qs = jnp.concatenate([jnp.where(low_half, qp, jnp.zeros_like(qp)),
                                  jnp.where(low_half, jnp.zeros_like(qp), qp)], axis=0)
            k_all = jnp.concatenate([k_ref[pl.ds(ks, NA_TK), lanes(p)], kc_ref[:, lanes(p)]], axis=0)
            s_scr[slot, p] = _scores(qs, k_all, bias_scr[2 * p:2 * p + 2].reshape(2 * NA_TQ, NA_TK))

    def stage_b(slot):
        for p in range(NA_PAIRS):
            m_scr[slot, p] = _row_max(s_scr[slot, p], None)

    def stage_c(slot):
        ks = window_start(n - 2)
        for p in range(NA_PAIRS):
            v_all = jnp.concatenate([v_ref[pl.ds(ks, NA_TK), lanes(p)], vc_ref[:, lanes(p)]], axis=0)
            pv_scr[slot, p] = _exp_pv(s_scr.at[slot, p], m_scr[slot, p], v_all)

    def stage_d(slot):
        for p in range(NA_PAIRS):
            pv = pv_scr[slot, p]
            out = pv[:, :LANES] / pv[:, LANES:]
            o_ref[:, lanes(p)] = jnp.where(low_half, out[:NA_TQ], out[NA_TQ:]).astype(o_ref.dtype)

    _pipeline_step(n, stage_a, stage_b, stage_c, stage_d)


def _na_attention(qa, ka, va, btab, with_ctx):
    n_blocks = NA_STEPS + (NA_CTX_STEPS if with_ctx else 0)
    rows = 2 * NA_TQ
    scratch = [
        pltpu.VMEM((NA_HEADS, NA_TQ, NA_TK), F32),
        pltpu.VMEM((ATTN_SLOTS, NA_PAIRS, rows, NA_NK), F32),
        pltpu.VMEM((ATTN_SLOTS, NA_PAIRS, rows, LANES), F32),
        pltpu.VMEM((ATTN_SLOTS, NA_PAIRS, rows, 2 * LANES), F32),
    ]
    bt_spec = [_const_spec((2 * NA_WIN_ROWS - 1, NA_HEADS, GRID_W, GRID_W))]
    return _attn_call(_na_kernel, "na_attn", (btab, qa, ka, va, ka, va), bt_spec,
                      NA_HEADS * HEAD_DIM, NA_TQ, NA_STEPS, n_blocks, scratch)


SW_NK = SW_TK + CTX_LEN
SW_KV = SW_HEADS // SW_GROUP


def _swa_kernel(sink_ref, q_ref, k_ref, v_ref, kc_ref, vc_ref, o_ref,
                mask_scr, s_scr, m_scr, pv_scr, *, n_blocks):
    n = pl.program_id(1)
    lane = _lane_iota()
    low_half = lane < HEAD_DIM
    group0 = (lane % HEAD_DIM) < HEAD_DIM // 2

    def window_row(j):
        return jnp.where(j < SW_STEPS, jnp.clip(j * SW_TQ - SW_TQ, 0, SEQ - SW_TK), 0)

    def window_start(chain):
        return pl.multiple_of(window_row(jnp.clip(chain, 0, n_blocks - 1)), SW_TQ)

    def sink_rows(g, width):
        return jnp.concatenate(
            [jnp.full((SW_TQ, width), sink_ref[p + SW_GROUP * g], F32) for p in range(SW_GROUP)], axis=0)

    def lanes(p):
        return slice(LANES * p, LANES * (p + 1))

    @pl.when(n == 0)
    def _init():
        _init_stage_buffers(s_scr, m_scr, pv_scr)

    jb = jnp.minimum(n, n_blocks - 1)

    @pl.when((jb <= 1) | (jb >= SW_STEPS - 1))
    def _regen():
        rel = ((window_row(jb) - jb * SW_TQ) + lax.broadcasted_iota(jnp.int32, (SW_TQ, SW_TK), 1)
               - lax.broadcasted_iota(jnp.int32, (SW_TQ, SW_TK), 0))
        band = jnp.where((jnp.abs(rel) <= SW_WINDOW) & (jb < SW_STEPS), 0.0, NEG)
        for p in range(SW_GROUP):
            mask_scr[SW_TQ * p:SW_TQ * (p + 1), :] = band

    def stage_a(slot):
        ks = window_start(n)
        k_all = jnp.concatenate([k_ref[pl.ds(ks, SW_TK), :], kc_ref[...]], axis=0)
        for g in range(SW_KV):
            in_group = group0 if g == 0 else ~group0
            qs = jnp.concatenate(
                [jnp.where(in_group, q_ref[:, lanes(p)], jnp.zeros((SW_TQ, LANES), BF16))
                 for p in range(SW_GROUP)], axis=0)
            s_scr[slot, g] = _scores(qs, k_all, mask_scr[...])

    def stage_b(slot):
        for g in range(SW_KV):
            m_scr[slot, g] = _row_max(s_scr[slot, g], sink_rows(g, 1))

    def stage_c(slot):
        ks = window_start(n - 2)
        v_all = jnp.concatenate([v_ref[pl.ds(ks, SW_TK), :], vc_ref[...]], axis=0)
        for g in range(SW_KV):
            m = m_scr[slot, g]
            pv = _exp_pv(s_scr.at[slot, g], m, v_all)
            pv_scr[slot, g, :, :LANES] = pv[:, :LANES]
            pv_scr[slot, g, :, LANES:] = pv[:, LANES:] + jnp.exp(sink_rows(g, LANES) - m)

    def stage_d(slot):
        outs = [pv_scr[slot, g, :, :LANES] / pv_scr[slot, g, :, LANES:] for g in range(SW_KV)]
        for p in range(SW_GROUP):
            blk = slice(SW_TQ * p, SW_TQ * (p + 1))
            o_ref[:, lanes(p)] = jnp.where(low_half, outs[0][blk], outs[1][blk]).astype(o_ref.dtype)

    _pipeline_step(n, stage_a, stage_b, stage_c, stage_d)


def _swa_attention(sink, qb, kb, vb, with_ctx):
    n_blocks = SW_STEPS + (SW_CTX_STEPS if with_ctx else 0)
    rows = SW_GROUP * SW_TQ
    scratch = [
        pltpu.VMEM((rows, SW_TK), F32),
        pltpu.VMEM((ATTN_SLOTS, SW_KV, rows, SW_NK), F32),
        pltpu.VMEM((ATTN_SLOTS, SW_KV, rows, LANES), F32),
        pltpu.VMEM((ATTN_SLOTS, SW_KV, rows, 2 * LANES), F32),
    ]
    return _attn_call(_swa_kernel, "swa_attn", (sink, qb, kb, vb, kb, vb),
                      [pl.BlockSpec(memory_space=pltpu.SMEM)], LANES, SW_TQ, SW_STEPS, n_blocks, scratch)


HALO = 16


def _merge_kernel(*refs, split):
    x_ref, xc_ref = (refs[0], refs[1]) if split else (refs[0], None)
    (a_ref, b_ref, u_ref, up_ref, un_ref, cgb_ref, gate_ref, g1_ref, cw_ref,
     wa_ref, wb_ref, wc_ref, wo_ref, o_ref) = refs[2 if split else 1:]
    i = pl.program_id(0)
    local = lax.broadcasted_iota(jnp.int32, (TM, 1), 0)
    row = i * TM + local
    is_lat = row < N_LAT
    pos = jnp.where(is_lat, row & (SEQ - 1), (row - N_LAT) & (CTX_LEN - 1))
    seq_first = pos == 0
    seq_last = pos == jnp.where(is_lat, SEQ - 1, CTX_LEN - 1)

    u = u_ref[...].astype(F32)
    prev_row = up_ref[...].astype(F32)[HALO - 1:HALO, :]
    next_row = un_ref[...].astype(F32)[0:1, :]
    u_prev = jnp.where(local == 0, prev_row, pltpu.roll(u, 1, 0))
    u_prev = jnp.where(seq_first, 0.0, u_prev)
    u_next = jnp.where(local == TM - 1, next_row, pltpu.roll(u, TM - 1, 0))
    u_next = jnp.where(seq_last, 0.0, u_next)
    cw = cw_ref[...]
    y = cw[0:1, :] * u_prev + cw[1:2, :] * u + cw[2:3, :] * u_next
    c_out = (cgb_ref[...].astype(F32) * y).astype(BF16)

    merged = gate_ref[:, 0:D_MODEL].astype(F32) * _dot(a_ref[...], wa_ref[...])
    merged = merged + gate_ref[:, D_MODEL:2 * D_MODEL].astype(F32) * _dot(b_ref[...], wb_ref[...])
    merged = merged + gate_ref[:, 2 * D_MODEL:3 * D_MODEL].astype(F32) * _dot(c_out, wc_ref[...])
    o_ref[...] = _load_tokens(x_ref, xc_ref) + g1_ref[...] * _dot(merged.astype(BF16), wo_ref[...])


def _merge(xs, a, b, u, cgb, gates, mods, conv_w, wa, wb, wc, wo, n_rows):
    halo_blocks = N_TOK // HALO
    per = TM // HALO
    split = len(xs) == 2

    def tok(wd):
        return pl.BlockSpec((TM, wd), lambda i: (i, 0))

    return pl.pallas_call(
        functools.partial(_merge_kernel, split=split),
        grid=(n_rows // TM,),
        in_specs=_token_specs(split) + [
            tok(512), tok(512), tok(CONV_WIDTH),
            pl.BlockSpec((HALO, CONV_WIDTH), lambda i: (jnp.maximum(i * per - 1, 0), 0)),
            pl.BlockSpec((HALO, CONV_WIDTH), lambda i: (jnp.minimum((i + 1) * per, halo_blocks - 1), 0)),
            tok(CONV_WIDTH), tok(3 * D_MODEL),
            _mod_spec(2),
            _const_spec((3, CONV_WIDTH)),
            _const_spec((512, D_MODEL)), _const_spec((512, D_MODEL)), _const_spec((512, D_MODEL)),
            _const_spec((D_MODEL, D_MODEL)),
        ],
        out_specs=tok(D_MODEL),
        out_shape=jax.ShapeDtypeStruct((n_rows, D_MODEL), F32),
        compiler_params=_cparams(1),
        name="merge",
    )(*xs, a, b, u, u, u, cgb, gates, mods, conv_w, wa, wb, wc, wo)


FF_CHUNKS = tuple((c, min(c + 512, D_FF)) for c in range(0, D_FF, 512))


def _ffn_kernel(x_ref, sh_ref, sc_ref, g2_ref, ng_ref, fg_ref, wu_ref, wd_ref, o_ref, *, final):
    x = x_ref[...]
    h = _mod_norm(x, ng_ref[...], sh_ref[...], sc_ref[...]).astype(BF16)
    acc = None
    for c0, c1 in FF_CHUNKS:
        gate = _dot(h, wu_ref[:, c0:c1])
        up = _dot(h, wu_ref[:, D_FF + c0:D_FF + c1])
        act = (gate * jax.nn.sigmoid(gate) * up).astype(BF16)
        part = _dot(act, wd_ref[c0:c1, :])
        acc = part if acc is None else acc + part
    xn = x + g2_ref[...] * acc
    if final:
        xn = xn * lax.rsqrt(jnp.mean(xn * xn, axis=-1, keepdims=True) + NORM_EPS) * fg_ref[...]
    o_ref[...] = xn


def _ffn(xf, mods, ng, fg, wu, wd, final):
    n_rows = xf.shape[0]
    return pl.pallas_call(
        functools.partial(_ffn_kernel, final=final),
        grid=(n_rows // TM,),
        in_specs=[
            pl.BlockSpec((TM, D_MODEL), lambda i: (i, 0)),
            _mod_spec(3), _mod_spec(4), _mod_spec(5),
            _const_spec((1, D_MODEL)), _const_spec((1, D_MODEL)),
            _const_spec((D_MODEL, 2 * D_FF)), _const_spec((D_FF, D_MODEL)),
        ],
        out_specs=pl.BlockSpec((TM, D_MODEL), lambda i: (i, 0)),
        out_shape=jax.ShapeDtypeStruct((n_rows, D_MODEL), F32),
        compiler_params=_cparams(1),
        name="ffn",
    )(xf, mods, mods, mods, ng, fg, wu, wd)


def _prep_w_rot(w_in):
    half = HEAD_DIM // 2
    lead = w_in.shape[:-1]
    qb = w_in[..., C_QB:C_KB].reshape(*lead, 2, SW_GROUP, 2, half)
    qb = jnp.moveaxis(qb, -4, -2).reshape(*lead, SW_HEADS * HEAD_DIM)
    kb = w_in[..., C_KB:C_VB].reshape(*lead, 2, 2, half)
    kb = jnp.swapaxes(kb, -3, -2).reshape(*lead, 2 * HEAD_DIM)
    return jnp.concatenate([qb, kb, w_in[..., C_VB:C_CX]], axis=-1).astype(BF16)


def _prep_w_br_b(w_br_b):
    w = w_br_b.reshape(DEPTH, 2, SW_GROUP, HEAD_DIM, D_MODEL)
    return jnp.swapaxes(w, 1, 2).reshape(DEPTH, SW_HEADS * HEAD_DIM, D_MODEL).astype(BF16)


def _rope_tables():
    t = np.arange(SEQ)
    row = (t // GRID_W).astype(np.float32)
    col = (t % GRID_W).astype(np.float32)
    inv = jnp.asarray(ROPE_THETA, F32) ** (-jnp.arange(ROPE_FREQS, dtype=F32) / ROPE_FREQS)
    ang = jnp.concatenate([jnp.asarray(row)[:, None] * inv, jnp.asarray(col)[:, None] * inv], axis=-1)
    cos = jnp.cos(ang)
    sin = jnp.sin(ang)
    cos4 = jnp.concatenate([cos, cos, cos, cos], axis=-1)
    sin4 = jnp.concatenate([-sin, -sin, sin, sin], axis=-1)
    cos4 = jnp.concatenate([cos4, jnp.ones((TM, LANES), F32)], axis=0)
    sin4 = jnp.concatenate([sin4, jnp.zeros((TM, LANES), F32)], axis=0)
    return cos4, sin4


def _bias_tiles(rpb):
    qc = np.arange(GRID_W)[:, None]
    kc = np.arange(GRID_W)[None, :]
    start = np.clip(qc - NA_WIN_COLS // 2, 0, GRID_W - NA_WIN_COLS)
    valid = (kc >= start) & (kc < start + NA_WIN_COLS)
    dcol = kc - qc + NA_WIN_COLS - 1
    onehot = ((dcol[None] == np.arange(2 * NA_WIN_COLS - 1)[:, None, None]) & valid[None]).astype(np.float32)
    tiles = jnp.einsum("lhrd,dqk->lrhqk", rpb.astype(F32), jnp.asarray(onehot),
                       precision=lax.Precision.HIGHEST)
    return tiles + jnp.asarray(np.where(valid, 0.0, NEG).astype(np.float32))


def kernel(x, c, ctx, c_ctx, w_ada, b_ada, norm1_g, norm2_g, w_in, b_gate, na_rpb, sw_sink, conv_w,
           w_br_a, w_br_b, w_br_c, w_o, w_ffn_up, w_ffn_down, final_norm_g):
    xs = (x.reshape(N_LAT, D_MODEL), ctx.reshape(N_CTX, D_MODEL))
    cvecs = jnp.concatenate([c, c_ctx[None, :], jnp.zeros((8 - BATCH - 1, D_MODEL), F32)], axis=0)
    mods_all = _ada(cvecs, w_ada, b_ada).reshape(DEPTH, 8, 1, 6 * D_MODEL)

    rope_c, rope_s = _rope_tables()
    final_g = final_norm_g.reshape(1, D_MODEL)
    w_in_b, w_rot_b = w_in.astype(BF16), _prep_w_rot(w_in)
    w_a_b, w_b_b, w_c_b = w_br_a.astype(BF16), _prep_w_br_b(w_br_b), w_br_c.astype(BF16)
    w_o_b, w_up_b, w_dn_b = w_o.astype(BF16), w_ffn_up.astype(BF16), w_ffn_down.astype(BF16)
    btab = _bias_tiles(na_rpb)

    for l in range(DEPTH):
        last = l == DEPTH - 1
        mods = mods_all[l]
        qa, ka, va, qb, kb, vb, u, cgb, gates = _inproj(
            xs, mods, norm1_g[l].reshape(1, D_MODEL), w_in_b[l], w_rot_b[l],
            b_gate[l].reshape(1, 3 * D_MODEL), rope_c, rope_s)
        a_out = _na_attention(qa, ka, va, btab[l], not last)
        b_out = _swa_attention(sw_sink[l], qb, kb, vb, not last)
        n_rows = N_LAT if last else N_TOK
        xf = _merge(xs, a_out, b_out, u, cgb, gates, mods, conv_w[l],
                    w_a_b[l], w_b_b[l], w_c_b[l], w_o_b[l], n_rows)
        xf = _ffn(xf, mods, norm2_g[l].reshape(1, D_MODEL), final_g, w_up_b[l], w_dn_b[l], last)
        xs = (xf,)
    return xf.reshape(BATCH, SEQ, D_MODEL)
```

```python
import functools

import numpy as np
import jax
import jax.numpy as jnp
from jax import lax
from jax.experimental import pallas as pl
from jax.experimental.pallas import tpu as pltpu

F32 = jnp.float32
BF16 = jnp.bfloat16

D_MODEL = 1024
BATCH = 4
SEQ = 8192
DEPTH = 4
GRID_W = 64
GRID_ROWS = SEQ // GRID_W
CTX_LEN = 256
HEAD_DIM = 64
NORM_EPS = 1e-6
NA_HEADS = 8
NA_WIN_ROWS = 8
NA_WIN_COLS = 16
SW_HEADS = 8
SW_WINDOW = 128
ROPE_THETA = 10000.0
ROPE_FREQS = HEAD_DIM // 4
CONV_WIDTH = 512
D_FF = 2816
N_IN = 6912
N_ROT = 768

N_LAT = BATCH * SEQ
N_CTX = BATCH * CTX_LEN
N_TOK = N_LAT + N_CTX

LANES = 128
NEG = -1e30

C_QA, C_KA, C_VA, C_QB, C_KB, C_VB, C_CX, C_CGB, C_CGC, C_ZG = (
    0, 512, 1024, 1536, 2048, 2176, 2304, 2816, 3328, 3840)

TM = 512
NA_QROWS = 2
NA_KROWS = NA_QROWS + NA_WIN_ROWS
NA_TQ = NA_QROWS * GRID_W
NA_TK = NA_KROWS * GRID_W
NA_STEPS = GRID_ROWS // NA_QROWS
NA_CTX_STEPS = CTX_LEN // NA_TQ
SW_TQ = 128
SW_TK = 3 * SW_TQ
SW_STEPS = SEQ // SW_TQ
SW_CTX_STEPS = CTX_LEN // SW_TQ
SW_GROUP = 4
VMEM_LIMIT = 56 * 1024 * 1024


def _cparams(n_axes):
    return pltpu.CompilerParams(dimension_semantics=("arbitrary",) * n_axes,
                                vmem_limit_bytes=VMEM_LIMIT)


def _const_spec(shape):
    nd = len(shape)
    return pl.BlockSpec(shape, lambda *_: (0,) * nd, pipeline_mode=pl.Buffered(1))


def _layer_spec(shape, layer):
    nd = len(shape)
    return pl.BlockSpec((None,) + tuple(shape), lambda *_: (layer,) + (0,) * nd, pipeline_mode=pl.Buffered(1))


def _dot(a, b):
    return jnp.dot(a, b, preferred_element_type=F32)


def _dot_nt(a, b):
    return lax.dot_general(a, b, (((1,), (1,)), ((), ())), preferred_element_type=F32)


def _mod_norm(x, g, shift, scale):
    y = x * lax.rsqrt(jnp.mean(x * x, axis=-1, keepdims=True) + NORM_EPS) * g
    return y * (1.0 + scale) + shift


def _ada_kernel(c_ref, w_ref, b_ref, o_ref):
    c = c_ref[...]
    sc = c * jax.nn.sigmoid(c)
    o_ref[...] = jnp.dot(sc, w_ref[...], preferred_element_type=F32,
                         precision=lax.Precision.HIGHEST) + b_ref[...]


def _ada(cvecs, w_ada, b_ada):
    tn = 1536
    return pl.pallas_call(
        _ada_kernel,
        grid=(DEPTH, 6 * D_MODEL // tn),
        in_specs=[
            pl.BlockSpec((8, D_MODEL), lambda l, n: (0, 0)),
            pl.BlockSpec((None, D_MODEL, tn), lambda l, n: (l, 0, n)),
            pl.BlockSpec((None, 1, tn), lambda l, n: (l, 0, n)),
        ],
        out_specs=pl.BlockSpec((None, 8, tn), lambda l, n: (l, 0, n)),
        out_shape=jax.ShapeDtypeStruct((DEPTH, 8, 6 * D_MODEL), F32),
        compiler_params=_cparams(2),
        name="ada",
    )(cvecs, w_ada, b_ada.reshape(DEPTH, 1, 6 * D_MODEL))


def _mod_row(i):
    lat_blocks = N_LAT // TM
    return jnp.where(i < lat_blocks, i // (SEQ // TM), BATCH)


def _mod_spec(part):
    return pl.BlockSpec((None, 1, D_MODEL), lambda i: (_mod_row(i), 0, part))


def _load_tokens(x_ref, xc_ref):
    if xc_ref is None:
        return x_ref[...]
    return jnp.where(pl.program_id(0) < N_LAT // TM, x_ref[...], xc_ref[...])


def _inproj_kernel(*refs, split):
    x_ref, xc_ref = (refs[0], refs[1]) if split else (refs[0], None)
    (sh_ref, sc_ref, g_ref, w_ref, wr_ref, bg_ref, cos_ref, sin_ref,
     qa_ref, ka_ref, va_ref, qb_ref, kb_ref, vb_ref, u_ref, cgb_ref, gate_ref) = refs[2 if split else 1:]
    h = _mod_norm(_load_tokens(x_ref, xc_ref), g_ref[...], sh_ref[...], sc_ref[...]).astype(BF16)
    q_scale = HEAD_DIM ** -0.5

    def proj(c0, c1):
        return _dot(h, w_ref[:, c0:c1])

    qa_ref[...] = (proj(C_QA, C_KA) * q_scale).astype(BF16)
    ka_ref[...] = proj(C_KA, C_VA).astype(BF16)
    va_ref[...] = proj(C_VA, C_QB).astype(BF16)

    cosv = cos_ref[...]
    sinv = sin_ref[...]

    def rope(t):
        return t * cosv + pltpu.roll(t, LANES // 2, 1) * sinv

    rot = _dot(h, wr_ref[...])
    for p in range(4):
        qb_ref[:, LANES * p:LANES * (p + 1)] = (
            rope(rot[:, LANES * p:LANES * (p + 1)]) * q_scale).astype(BF16)
    kb_ref[...] = rope(rot[:, 4 * LANES:5 * LANES]).astype(BF16)
    vb_ref[...] = rot[:, 5 * LANES:].astype(BF16)

    u_ref[...] = (proj(C_CGC, C_ZG) * proj(C_CX, C_CGB)).astype(BF16)
    cgb_ref[...] = proj(C_CGB, C_CGC).astype(BF16)
    for k in range(3):
        z = proj(C_ZG + D_MODEL * k, C_ZG + D_MODEL * (k + 1))
        gate_ref[:, D_MODEL * k:D_MODEL * (k + 1)] = jax.nn.sigmoid(
            z + bg_ref[:, D_MODEL * k:D_MODEL * (k + 1)]).astype(BF16)


def _token_specs(split):
    lat_blocks = N_LAT // TM
    if not split:
        return [pl.BlockSpec((TM, D_MODEL), lambda i: (i, 0))]
    return [pl.BlockSpec((TM, D_MODEL), lambda i: (jnp.minimum(i, lat_blocks - 1), 0)),
            pl.BlockSpec((TM, D_MODEL), lambda i: (jnp.maximum(i - lat_blocks, 0), 0))]


def _inproj(xs, mods, g, w, w_rot, bg, rope_c, rope_s, layer):
    lat_blocks = N_LAT // TM
    seq_blocks = SEQ // TM
    split = len(xs) == 2

    def rope_idx(i):
        return (jnp.where(i < lat_blocks, i % seq_blocks, seq_blocks), 0)

    widths = (512, 512, 512, 512, 128, 128, 512, 512, 3072)
    return pl.pallas_call(
        functools.partial(_inproj_kernel, split=split),
        grid=(N_TOK // TM,),
        in_specs=_token_specs(split) + [
            _mod_spec(0), _mod_spec(1),
            _const_spec((1, D_MODEL)),
            _layer_spec((D_MODEL, N_IN), layer),
            _layer_spec((D_MODEL, N_ROT), layer),
            _const_spec((1, 3 * D_MODEL)),
            pl.BlockSpec((TM, LANES), rope_idx),
            pl.BlockSpec((TM, LANES), rope_idx),
        ],
        out_specs=[pl.BlockSpec((TM, wd), lambda i: (i, 0)) for wd in widths],
        out_shape=[jax.ShapeDtypeStruct((N_TOK, wd), BF16) for wd in widths],
        compiler_params=_cparams(1),
        name="inproj",
    )(*xs, mods, mods, g, w, w_rot, bg, rope_c, rope_s)


ATTN_SUB = 2
ATTN_SLOTS = 4
ATTN_LAG = 4
ATTN_C_LAG = 2


def _lane_iota():
    return lax.broadcasted_iota(jnp.int32, (1, LANES), 1)


def _pipeline_step(step, stage_a, stage_b, stage_c, stage_d):
    for parity in range(ATTN_SLOTS // ATTN_SUB):
        @pl.when(step % (ATTN_SLOTS // ATTN_SUB) == parity)
        def _step(parity=parity):
            for k in range(ATTN_SUB):
                n = ATTN_SUB * step + k
                slot = (ATTN_SUB * parity + k) % ATTN_SLOTS
                stage_d(slot, n, k)
                stage_c((slot + ATTN_SLOTS - ATTN_C_LAG) % ATTN_SLOTS, n, k)
                stage_b((slot + ATTN_SLOTS - 1) % ATTN_SLOTS, n, k)
                stage_a(slot, n, k)


def _init_stage_buffers(s_scr, m_scr, pv_scr):
    s_scr[...] = jnp.zeros_like(s_scr)
    m_scr[...] = jnp.zeros_like(m_scr)
    pv_scr[...] = jnp.ones_like(pv_scr)


def _scores(qs, k_all, bias):
    s = _dot_nt(qs, k_all)
    nb = bias.shape[1]
    return jnp.concatenate([s[:, :nb] + bias, s[:, nb:]], axis=1)


def _row_max(s, sink):
    m = jnp.max(s, axis=-1, keepdims=True)
    if sink is not None:
        m = jnp.maximum(m, sink)
    return jnp.broadcast_to(m, (s.shape[0], LANES))


def _exp_pv(s_ref, m, v_all):
    nk = v_all.shape[0]
    e = jnp.concatenate(
        [jnp.exp(s_ref[:, LANES * c:LANES * (c + 1)] - m) for c in range(nk // LANES)], axis=1).astype(BF16)
    return _dot(e, jnp.concatenate([v_all, jnp.ones_like(v_all)], axis=1))


def _attn_row_block(b, j, lat_steps, tq):
    lat = b * lat_steps + j
    ctx = N_LAT // tq + b * (CTX_LEN // tq) + (j - lat_steps)
    return jnp.where(j < lat_steps, lat, ctx)


def _batch_spec(rows, width):
    return pl.BlockSpec((rows, width), lambda b, n: (b, 0), pipeline_mode=pl.Buffered(1))


def _ctx_spec(width):
    return pl.BlockSpec((CTX_LEN, width), lambda b, n: (N_LAT // CTX_LEN + b, 0))


def _attn_call(kernel_fn, name, operands, lead_specs, kv_specs, kv_width, tq, lat_steps, n_blocks, scratch):
    width = NA_HEADS * HEAD_DIM
    n_out = N_LAT + (n_blocks - lat_steps) * tq * BATCH
    n_steps = n_blocks // ATTN_SUB
    lag_steps = ATTN_LAG // ATTN_SUB
    tq = tq * ATTN_SUB
    lat_steps = lat_steps // ATTN_SUB

    def q_idx(b, n):
        return (_attn_row_block(b, jnp.minimum(n, n_steps - 1), lat_steps, tq), 0)

    def o_idx(b, n):
        return (_attn_row_block(b, jnp.maximum(n - lag_steps, 0), lat_steps, tq), 0)

    return pl.pallas_call(
        functools.partial(kernel_fn, n_blocks=n_blocks),
        grid=(BATCH, n_steps + lag_steps),
        in_specs=lead_specs + [
            pl.BlockSpec((tq, width), q_idx),
            *kv_specs,
            _ctx_spec(kv_width), _ctx_spec(kv_width),
        ],
        out_specs=pl.BlockSpec((tq, width), o_idx),
        out_shape=jax.ShapeDtypeStruct((n_out, width), BF16),
        scratch_shapes=scratch,
        compiler_params=_cparams(2),
        name=name,
    )(*operands)


NA_PAIRS = NA_HEADS // 2
NA_NK = NA_TK + CTX_LEN
NA_UROWS = ATTN_SUB * NA_QROWS + NA_WIN_ROWS


def _na_union_row(step):
    return jnp.clip(ATTN_SUB * NA_QROWS * step - NA_WIN_ROWS // 2, 0, GRID_ROWS - NA_UROWS)


def _na_kernel(bt_ref, q_ref, k_ref, v_ref, kc_ref, vc_ref, o_ref,
               bias_scr, s_scr, m_scr, pv_scr, *, n_blocks):
    step = pl.program_id(1)
    low_half = _lane_iota() < HEAD_DIM

    def window_row(j):
        return jnp.where(j < NA_STEPS,
                         jnp.clip(NA_QROWS * j - NA_WIN_ROWS // 2, 0, GRID_ROWS - NA_KROWS), 0)

    def window_start(chain, lag):
        kr0 = window_row(jnp.clip(chain, 0, n_blocks - 1))
        union0 = _na_union_row(jnp.maximum(step - lag, 0))
        return pl.multiple_of(jnp.clip(kr0 - union0, 0, NA_UROWS - NA_KROWS) * GRID_W, GRID_W)

    @pl.when(step == 0)
    def _init():
        _init_stage_buffers(s_scr, m_scr, pv_scr)

    first = jnp.minimum(ATTN_SUB * step, n_blocks - ATTN_SUB)

    @pl.when((first <= 2) | (first >= NA_STEPS - 2))
    def _regen():
        for k in range(ATTN_SUB):
            jb = first + k
            kr0 = window_row(jb)
            for i in range(NA_QROWS):
                qr = NA_QROWS * jb + i
                r0 = jnp.clip(qr - NA_WIN_ROWS // 2, 0, GRID_ROWS - NA_WIN_ROWS)
                for jj in range(0, NA_KROWS, 2):
                    tiles = []
                    for kk in (jj, jj + 1):
                        kr = kr0 + kk
                        valid = (kr >= r0) & (kr < r0 + NA_WIN_ROWS) & (jb < NA_STEPS)
                        idx = jnp.clip(kr - qr + NA_WIN_ROWS - 1, 0, 2 * NA_WIN_ROWS - 2)
                        tiles.append(jnp.where(valid, bt_ref[idx], NEG))
                    bias_scr[k, :, GRID_W * i:GRID_W * (i + 1), GRID_W * jj:GRID_W * (jj + 2)] = (
                        jnp.concatenate(tiles, axis=-1))

    def lanes(p):
        return slice(LANES * p, LANES * (p + 1))

    def sub_rows(k):
        return slice(NA_TQ * k, NA_TQ * (k + 1))

    def stage_a(slot, n, k):
        ks = window_start(n, 0)
        for p in range(NA_PAIRS):
            qp = q_ref[sub_rows(k), lanes(p)]
            qs = jnp.concatenate([jnp.where(low_half, qp, jnp.zeros_like(qp)),
                                  jnp.where(low_half, jnp.zeros_like(qp), qp)], axis=0)
            k_all = jnp.concatenate([k_ref[pl.ds(ks, NA_TK), lanes(p)], kc_ref[:, lanes(p)]], axis=0)
            s_scr[slot, p] = _scores(qs, k_all, bias_scr[k, 2 * p:2 * p + 2].reshape(2 * NA_TQ, NA_TK))

    def stage_b(slot, n, k):
        for p in range(NA_PAIRS):
            m_scr[slot, p] = _row_max(s_scr[slot, p], None)

    def stage_c(slot, n, k):
        ks = window_start(n - ATTN_C_LAG, ATTN_C_LAG // ATTN_SUB)
        for p in range(NA_PAIRS):
            v_all = jnp.concatenate([v_ref[pl.ds(ks, NA_TK), lanes(p)], vc_ref[:, lanes(p)]], axis=0)
            pv_scr[slot, p] = _exp_pv(s_scr.at[slot, p], m_scr[slot, p], v_all)

    def stage_d(slot, n, k):
        for p in range(NA_PAIRS):
            pv = pv_scr[slot, p]
            out = pv[:, :LANES] / pv[:, LANES:]
            o_ref[sub_rows(k), lanes(p)] = jnp.where(low_half, out[:NA_TQ], out[NA_TQ:]).astype(o_ref.dtype)

    _pipeline_step(step, stage_a, stage_b, stage_c, stage_d)


def _na_attention(qa, ka, va, btab, with_ctx, layer):
    n_blocks = NA_STEPS + (NA_CTX_STEPS if with_ctx else 0)
    rows = 2 * NA_TQ
    scratch = [
        pltpu.VMEM((ATTN_SUB, NA_HEADS, NA_TQ, NA_TK), F32),
        pltpu.VMEM((ATTN_SLOTS, NA_PAIRS, rows, NA_NK), F32),
        pltpu.VMEM((ATTN_SLOTS, NA_PAIRS, rows, LANES), F32),
        pltpu.VMEM((ATTN_SLOTS, NA_PAIRS, rows, 2 * LANES), F32),
    ]
    bt_spec = [_layer_spec((2 * NA_WIN_ROWS - 1, NA_HEADS, GRID_W, GRID_W), layer)]
    width = NA_HEADS * HEAD_DIM

    def window_spec(lag):
        return pl.BlockSpec(
            (pl.Element(NA_UROWS * GRID_W), pl.Element(width)),
            lambda b, s: (pl.multiple_of(b * SEQ + _na_union_row(jnp.maximum(s - lag, 0)) * GRID_W, GRID_W), 0))

    return _attn_call(_na_kernel, "na_attn", (btab, qa, ka, va, ka, va), bt_spec,
                      [window_spec(0), window_spec(ATTN_C_LAG // ATTN_SUB)],
                      width, NA_TQ, NA_STEPS, n_blocks, scratch)


SW_NK = SW_TK + CTX_LEN
SW_KV = SW_HEADS // SW_GROUP


def _swa_kernel(sink_ref, q_ref, k_ref, v_ref, kc_ref, vc_ref, o_ref,
                mask_scr, s_scr, m_scr, pv_scr, *, n_blocks):
    step = pl.program_id(1)
    lane = _lane_iota()
    low_half = lane < HEAD_DIM
    group0 = (lane % HEAD_DIM) < HEAD_DIM // 2

    def window_row(j):
        return jnp.where(j < SW_STEPS, jnp.clip(j * SW_TQ - SW_TQ, 0, SEQ - SW_TK), 0)

    def window_start(chain):
        return pl.multiple_of(window_row(jnp.clip(chain, 0, n_blocks - 1)), SW_TQ)

    def sink_rows(g, width):
        return jnp.concatenate(
            [jnp.full((SW_TQ, width), sink_ref[p + SW_GROUP * g], F32) for p in range(SW_GROUP)], axis=0)

    def lanes(p):
        return slice(LANES * p, LANES * (p + 1))

    @pl.when(step == 0)
    def _init():
        _init_stage_buffers(s_scr, m_scr, pv_scr)

    first = jnp.minimum(ATTN_SUB * step, n_blocks - ATTN_SUB)

    @pl.when((first <= 2) | (first >= SW_STEPS - 2))
    def _regen():
        for k in range(ATTN_SUB):
            jb = first + k
            rel = ((window_row(jb) - jb * SW_TQ) + lax.broadcasted_iota(jnp.int32, (SW_TQ, SW_TK), 1)
                   - lax.broadcasted_iota(jnp.int32, (SW_TQ, SW_TK), 0))
            band = jnp.where((jnp.abs(rel) <= SW_WINDOW) & (jb < SW_STEPS), 0.0, NEG)
            for p in range(SW_GROUP):
                mask_scr[k, SW_TQ * p:SW_TQ * (p + 1), :] = band

    def sub_rows(k):
        return slice(SW_TQ * k, SW_TQ * (k + 1))

    def stage_a(slot, n, k):
        ks = window_start(n)
        k_all = jnp.concatenate([k_ref[pl.ds(ks, SW_TK), :], kc_ref[...]], axis=0)
        for g in range(SW_KV):
            in_group = group0 if g == 0 else ~group0
            qs = jnp.concatenate(
                [jnp.where(in_group, q_ref[sub_rows(k), lanes(p)], jnp.zeros((SW_TQ, LANES), BF16))
                 for p in range(SW_GROUP)], axis=0)
            s_scr[slot, g] = _scores(qs, k_all, mask_scr[k])

    def stage_b(slot, n, k):
        for g in range(SW_KV):
            m_scr[slot, g] = _row_max(s_scr[slot, g], sink_rows(g, 1))

    def stage_c(slot, n, k):
        ks = window_start(n - ATTN_C_LAG)
        v_all = jnp.concatenate([v_ref[pl.ds(ks, SW_TK), :], vc_ref[...]], axis=0)
        for g in range(SW_KV):
            m = m_scr[slot, g]
            pv = _exp_pv(s_scr.at[slot, g], m, v_all)
            pv_scr[slot, g, :, :LANES] = pv[:, :LANES]
            pv_scr[slot, g, :, LANES:] = pv[:, LANES:] + jnp.exp(sink_rows(g, LANES) - m)

    def stage_d(slot, n, k):
        outs = [pv_scr[slot, g, :, :LANES] / pv_scr[slot, g, :, LANES:] for g in range(SW_KV)]
        for p in range(SW_GROUP):
            blk = slice(SW_TQ * p, SW_TQ * (p + 1))
            o_ref[sub_rows(k), lanes(p)] = jnp.where(low_half, outs[0][blk], outs[1][blk]).astype(o_ref.dtype)

    _pipeline_step(step, stage_a, stage_b, stage_c, stage_d)


def _swa_attention(sink, qb, kb, vb, with_ctx):
    n_blocks = SW_STEPS + (SW_CTX_STEPS if with_ctx else 0)
    rows = SW_GROUP * SW_TQ
    scratch = [
        pltpu.VMEM((ATTN_SUB, rows, SW_TK), F32),
        pltpu.VMEM((ATTN_SLOTS, SW_KV, rows, SW_NK), F32),
        pltpu.VMEM((ATTN_SLOTS, SW_KV, rows, LANES), F32),
        pltpu.VMEM((ATTN_SLOTS, SW_KV, rows, 2 * LANES), F32),
    ]
    return _attn_call(_swa_kernel, "swa_attn", (sink, qb, kb, vb, kb, vb),
                      [pl.BlockSpec(memory_space=pltpu.SMEM)], [_batch_spec(SEQ, LANES)] * 2,
                      LANES, SW_TQ, SW_STEPS, n_blocks, scratch)


HALO = 16


def _merge_kernel(*refs, split):
    x_ref, xc_ref = (refs[0], refs[1]) if split else (refs[0], None)
    (a_ref, b_ref, u_ref, up_ref, un_ref, cgb_ref, gate_ref, g1_ref, cw_ref,
     wa_ref, wb_ref, wc_ref, wo_ref, o_ref) = refs[2 if split else 1:]
    i = pl.program_id(0)
    local = lax.broadcasted_iota(jnp.int32, (TM, 1), 0)
    row = i * TM + local
    is_lat = row < N_LAT
    pos = jnp.where(is_lat, row & (SEQ - 1), (row - N_LAT) & (CTX_LEN - 1))
    seq_first = pos == 0
    seq_last = pos == jnp.where(is_lat, SEQ - 1, CTX_LEN - 1)

    u = u_ref[...].astype(F32)
    prev_row = up_ref[...].astype(F32)[HALO - 1:HALO, :]
    next_row = un_ref[...].astype(F32)[0:1, :]
    u_prev = jnp.where(local == 0, prev_row, pltpu.roll(u, 1, 0))
    u_prev = jnp.where(seq_first, 0.0, u_prev)
    u_next = jnp.where(local == TM - 1, next_row, pltpu.roll(u, TM - 1, 0))
    u_next = jnp.where(seq_last, 0.0, u_next)
    cw = cw_ref[...]
    y = cw[0:1, :] * u_prev + cw[1:2, :] * u + cw[2:3, :] * u_next
    c_out = (cgb_ref[...].astype(F32) * y).astype(BF16)

    merged = gate_ref[:, 0:D_MODEL].astype(F32) * _dot(a_ref[...], wa_ref[...])
    merged = merged + gate_ref[:, D_MODEL:2 * D_MODEL].astype(F32) * _dot(b_ref[...], wb_ref[...])
    merged = merged + gate_ref[:, 2 * D_MODEL:3 * D_MODEL].astype(F32) * _dot(c_out, wc_ref[...])
    o_ref[...] = _load_tokens(x_ref, xc_ref) + g1_ref[...] * _dot(merged.astype(BF16), wo_ref[...])


def _merge(xs, a, b, u, cgb, gates, mods, conv_w, wa, wb, wc, wo, n_rows, layer):
    halo_blocks = N_TOK // HALO
    per = TM // HALO
    split = len(xs) == 2

    def tok(wd):
        return pl.BlockSpec((TM, wd), lambda i: (i, 0))

    return pl.pallas_call(
        functools.partial(_merge_kernel, split=split),
        grid=(n_rows // TM,),
        in_specs=_token_specs(split) + [
            tok(512), tok(512), tok(CONV_WIDTH),
            pl.BlockSpec((HALO, CONV_WIDTH), lambda i: (jnp.maximum(i * per - 1, 0), 0)),
            pl.BlockSpec((HALO, CONV_WIDTH), lambda i: (jnp.minimum((i + 1) * per, halo_blocks - 1), 0)),
            tok(CONV_WIDTH), tok(3 * D_MODEL),
            _mod_spec(2),
            _const_spec((3, CONV_WIDTH)),
            _layer_spec((512, D_MODEL), layer), _layer_spec((512, D_MODEL), layer),
            _layer_spec((512, D_MODEL), layer), _layer_spec((D_MODEL, D_MODEL), layer),
        ],
        out_specs=tok(D_MODEL),
        out_shape=jax.ShapeDtypeStruct((n_rows, D_MODEL), F32),
        compiler_params=_cparams(1),
        name="merge",
    )(*xs, a, b, u, u, u, cgb, gates, mods, conv_w, wa, wb, wc, wo)


FF_CHUNKS = tuple((c, min(c + 512, D_FF)) for c in range(0, D_FF, 512))


def _ffn_kernel(x_ref, sh_ref, sc_ref, g2_ref, ng_ref, fg_ref, wu_ref, wd_ref, o_ref, *, final):
    x = x_ref[...]
    h = _mod_norm(x, ng_ref[...], sh_ref[...], sc_ref[...]).astype(BF16)
    acc = None
    for c0, c1 in FF_CHUNKS:
        gate = _dot(h, wu_ref[:, c0:c1])
        up = _dot(h, wu_ref[:, D_FF + c0:D_FF + c1])
        act = (gate * jax.nn.sigmoid(gate) * up).astype(BF16)
        part = _dot(act, wd_ref[c0:c1, :])
        acc = part if acc is None else acc + part
    xn = x + g2_ref[...] * acc
    if final:
        xn = xn * lax.rsqrt(jnp.mean(xn * xn, axis=-1, keepdims=True) + NORM_EPS) * fg_ref[...]
    o_ref[...] = xn


def _ffn(xf, mods, ng, fg, wu, wd, final, layer):
    n_rows = xf.shape[0]
    return pl.pallas_call(
        functools.partial(_ffn_kernel, final=final),
        grid=(n_rows // TM,),
        in_specs=[
            pl.BlockSpec((TM, D_MODEL), lambda i: (i, 0)),
            _mod_spec(3), _mod_spec(4), _mod_spec(5),
            _const_spec((1, D_MODEL)), _const_spec((1, D_MODEL)),
            _layer_spec((D_MODEL, 2 * D_FF), layer), _layer_spec((D_FF, D_MODEL), layer),
        ],
        out_specs=pl.BlockSpec((TM, D_MODEL), lambda i: (i, 0)),
        out_shape=jax.ShapeDtypeStruct((n_rows, D_MODEL), F32),
        compiler_params=_cparams(1),
        name="ffn",
    )(xf, mods, mods, mods, ng, fg, wu, wd)


def _prep_w_rot(w_in):
    half = HEAD_DIM // 2
    lead = w_in.shape[:-1]
    qb = w_in[..., C_QB:C_KB].reshape(*lead, 2, SW_GROUP, 2, half)
    qb = jnp.moveaxis(qb, -4, -2).reshape(*lead, SW_HEADS * HEAD_DIM)
    kb = w_in[..., C_KB:C_VB].reshape(*lead, 2, 2, half)
    kb = jnp.swapaxes(kb, -3, -2).reshape(*lead, 2 * HEAD_DIM)
    return jnp.concatenate([qb, kb, w_in[..., C_VB:C_CX]], axis=-1).astype(BF16)


def _prep_w_br_b(w_br_b):
    w = w_br_b.reshape(DEPTH, 2, SW_GROUP, HEAD_DIM, D_MODEL)
    return jnp.swapaxes(w, 1, 2).reshape(DEPTH, SW_HEADS * HEAD_DIM, D_MODEL).astype(BF16)


def _rope_tables():
    t = np.arange(SEQ)
    row = (t // GRID_W).astype(np.float32)
    col = (t % GRID_W).astype(np.float32)
    inv = jnp.asarray(ROPE_THETA, F32) ** (-jnp.arange(ROPE_FREQS, dtype=F32) / ROPE_FREQS)
    ang = jnp.concatenate([jnp.asarray(row)[:, None] * inv, jnp.asarray(col)[:, None] * inv], axis=-1)
    cos = jnp.cos(ang)
    sin = jnp.sin(ang)
    cos4 = jnp.concatenate([cos, cos, cos, cos], axis=-1)
    sin4 = jnp.concatenate([-sin, -sin, sin, sin], axis=-1)
    cos4 = jnp.concatenate([cos4, jnp.ones((TM, LANES), F32)], axis=0)
    sin4 = jnp.concatenate([sin4, jnp.zeros((TM, LANES), F32)], axis=0)
    return cos4, sin4


def _bias_tiles(rpb):
    qc = np.arange(GRID_W)[:, None]
    kc = np.arange(GRID_W)[None, :]
    start = np.clip(qc - NA_WIN_COLS // 2, 0, GRID_W - NA_WIN_COLS)
    valid = (kc >= start) & (kc < start + NA_WIN_COLS)
    dcol = kc - qc + NA_WIN_COLS - 1
    onehot = ((dcol[None] == np.arange(2 * NA_WIN_COLS - 1)[:, None, None]) & valid[None]).astype(np.float32)
    tiles = jnp.einsum("lhrd,dqk->lrhqk", rpb.astype(F32), jnp.asarray(onehot),
                       precision=lax.Precision.HIGHEST)
    return tiles + jnp.asarray(np.where(valid, 0.0, NEG).astype(np.float32))


def kernel(x, c, ctx, c_ctx, w_ada, b_ada, norm1_g, norm2_g, w_in, b_gate, na_rpb, sw_sink, conv_w,
           w_br_a, w_br_b, w_br_c, w_o, w_ffn_up, w_ffn_down, final_norm_g):
    xs = (x.reshape(N_LAT, D_MODEL), ctx.reshape(N_CTX, D_MODEL))
    cvecs = jnp.concatenate([c, c_ctx[None, :], jnp.zeros((8 - BATCH - 1, D_MODEL), F32)], axis=0)
    mods_all = _ada(cvecs, w_ada, b_ada).reshape(DEPTH, 8, 1, 6 * D_MODEL)

    rope_c, rope_s = _rope_tables()
    final_g = final_norm_g.reshape(1, D_MODEL)
    w_in_b, w_rot_b = w_in.astype(BF16), _prep_w_rot(w_in)
    w_a_b, w_b_b, w_c_b = w_br_a.astype(BF16), _prep_w_br_b(w_br_b), w_br_c.astype(BF16)
    w_o_b, w_up_b, w_dn_b = w_o.astype(BF16), w_ffn_up.astype(BF16), w_ffn_down.astype(BF16)
    btab = _bias_tiles(na_rpb)

    for l in range(DEPTH):
        last = l == DEPTH - 1
        mods = mods_all[l]
        qa, ka, va, qb, kb, vb, u, cgb, gates = _inproj(
            xs, mods, norm1_g[l].reshape(1, D_MODEL), w_in_b, w_rot_b,
            b_gate[l].reshape(1, 3 * D_MODEL), rope_c, rope_s, l)
        a_out = _na_attention(qa, ka, va, btab, not last, l)
        b_out = _swa_attention(sw_sink[l], qb, kb, vb, not last)
        n_rows = N_LAT if last else N_TOK
        xf = _merge(xs, a_out, b_out, u, cgb, gates, mods, conv_w[l],
                    w_a_b, w_b_b, w_c_b, w_o_b, n_rows, l)
        xf = _ffn(xf, mods, norm2_g[l].reshape(1, D_MODEL), final_g, w_up_b, w_dn_b, last, l)
        xs = (xf,)
    return xf.reshape(BATCH, SEQ, D_MODEL)
```

```python
import functools

import numpy as np
import jax
import jax.numpy as jnp
from jax import lax
from jax.experimental import pallas as pl
from jax.experimental.pallas import tpu as pltpu

F32 = jnp.float32
BF16 = jnp.bfloat16

D_MODEL = 1024
BATCH = 4
SEQ = 8192
DEPTH = 4
GRID_W = 64
GRID_ROWS = SEQ // GRID_W
CTX_LEN = 256
HEAD_DIM = 64
NORM_EPS = 1e-6
NA_HEADS = 8
NA_WIN_ROWS = 8
NA_WIN_COLS = 16
SW_HEADS = 8
SW_WINDOW = 128
ROPE_THETA = 10000.0
ROPE_FREQS = HEAD_DIM // 4
CONV_WIDTH = 512
D_FF = 2816
N_IN = 6912
N_ROT = 768

N_LAT = BATCH * SEQ
N_CTX = BATCH * CTX_LEN
N_TOK = N_LAT + N_CTX

LANES = 128
NEG = -1e30

C_QA, C_KA, C_VA, C_QB, C_KB, C_VB, C_CX, C_CGB, C_CGC, C_ZG = (
    0, 512, 1024, 1536, 2048, 2176, 2304, 2816, 3328, 3840)

TM = 512
MERGE_TM = 1024
FFN_TM = 1024
NA_QROWS = 2
NA_KROWS = NA_QROWS + NA_WIN_ROWS
NA_TQ = NA_QROWS * GRID_W
NA_TK = NA_KROWS * GRID_W
NA_STEPS = GRID_ROWS // NA_QROWS
NA_CTX_STEPS = CTX_LEN // NA_TQ
SW_TQ = 128
SW_TK = 3 * SW_TQ
SW_STEPS = SEQ // SW_TQ
SW_CTX_STEPS = CTX_LEN // SW_TQ
SW_GROUP = 4
VMEM_LIMIT = 56 * 1024 * 1024


def _cparams(n_axes):
    return pltpu.CompilerParams(dimension_semantics=("arbitrary",) * n_axes,
                                vmem_limit_bytes=VMEM_LIMIT)


def _const_spec(shape):
    nd = len(shape)
    return pl.BlockSpec(shape, lambda *_: (0,) * nd, pipeline_mode=pl.Buffered(1))


def _layer_spec(shape, layer):
    nd = len(shape)
    return pl.BlockSpec((None,) + tuple(shape), lambda *_: (layer,) + (0,) * nd, pipeline_mode=pl.Buffered(1))


def _dot(a, b):
    return jnp.dot(a, b, preferred_element_type=F32)


def _dot_nt(a, b):
    return lax.dot_general(a, b, (((1,), (1,)), ((), ())), preferred_element_type=F32)


def _mod_norm(x, g, shift, scale):
    y = x * lax.rsqrt(jnp.mean(x * x, axis=-1, keepdims=True) + NORM_EPS) * g
    return y * (1.0 + scale) + shift


def _ada_kernel(c_ref, w_ref, b_ref, o_ref):
    c = c_ref[...]
    sc = c * jax.nn.sigmoid(c)
    o_ref[...] = jnp.dot(sc, w_ref[...], preferred_element_type=F32,
                         precision=lax.Precision.HIGHEST) + b_ref[...]


def _ada(cvecs, w_ada, b_ada):
    tn = 1536
    return pl.pallas_call(
        _ada_kernel,
        grid=(DEPTH, 6 * D_MODEL // tn),
        in_specs=[
            pl.BlockSpec((8, D_MODEL), lambda l, n: (0, 0)),
            pl.BlockSpec((None, D_MODEL, tn), lambda l, n: (l, 0, n)),
            pl.BlockSpec((None, 1, tn), lambda l, n: (l, 0, n)),
        ],
        out_specs=pl.BlockSpec((None, 8, tn), lambda l, n: (l, 0, n)),
        out_shape=jax.ShapeDtypeStruct((DEPTH, 8, 6 * D_MODEL), F32),
        compiler_params=_cparams(2),
        name="ada",
    )(cvecs, w_ada, b_ada.reshape(DEPTH, 1, 6 * D_MODEL))


def _mod_row(i, tm):
    lat_blocks = N_LAT // tm
    return jnp.where(i < lat_blocks, i // (SEQ // tm), BATCH)


def _mod_spec(part, tm=TM):
    return pl.BlockSpec((None, 1, D_MODEL), lambda i: (_mod_row(i, tm), 0, part))


def _load_tokens(x_ref, xc_ref):
    if xc_ref is None:
        return x_ref[...]
    return jnp.where(pl.program_id(0) < N_LAT // x_ref.shape[0], x_ref[...], xc_ref[...])


def _inproj_kernel(*refs, split):
    x_ref, xc_ref = (refs[0], refs[1]) if split else (refs[0], None)
    (sh_ref, sc_ref, g_ref, w_ref, wr_ref, bg_ref, cos_ref, sin_ref,
     qa_ref, ka_ref, va_ref, qb_ref, kb_ref, vb_ref, u_ref, cgb_ref, gate_ref) = refs[2 if split else 1:]
    h = _mod_norm(_load_tokens(x_ref, xc_ref), g_ref[...], sh_ref[...], sc_ref[...]).astype(BF16)
    q_scale = HEAD_DIM ** -0.5

    def proj(c0, c1):
        return _dot(h, w_ref[:, c0:c1])

    qa_ref[...] = (proj(C_QA, C_KA) * q_scale).astype(BF16)
    ka_ref[...] = proj(C_KA, C_VA).astype(BF16)
    va_ref[...] = proj(C_VA, C_QB).astype(BF16)

    cosv = cos_ref[...]
    sinv = sin_ref[...]

    def rope(t):
        return t * cosv + pltpu.roll(t, LANES // 2, 1) * sinv

    rot = _dot(h, wr_ref[...])
    for p in range(4):
        qb_ref[:, LANES * p:LANES * (p + 1)] = (
            rope(rot[:, LANES * p:LANES * (p + 1)]) * q_scale).astype(BF16)
    kb_ref[...] = rope(rot[:, 4 * LANES:5 * LANES]).astype(BF16)
    vb_ref[...] = rot[:, 5 * LANES:].astype(BF16)

    u_ref[...] = (proj(C_CGC, C_ZG) * proj(C_CX, C_CGB)).astype(BF16)
    cgb_ref[...] = proj(C_CGB, C_CGC).astype(BF16)
    for k in range(3):
        z = proj(C_ZG + D_MODEL * k, C_ZG + D_MODEL * (k + 1))
        gate_ref[:, D_MODEL * k:D_MODEL * (k + 1)] = jax.nn.sigmoid(
            z + bg_ref[:, D_MODEL * k:D_MODEL * (k + 1)]).astype(BF16)


def _token_specs(split, tm=TM):
    lat_blocks = N_LAT // tm
    if not split:
        return [pl.BlockSpec((tm, D_MODEL), lambda i: (i, 0))]
    return [pl.BlockSpec((tm, D_MODEL), lambda i: (jnp.minimum(i, lat_blocks - 1), 0)),
            pl.BlockSpec((tm, D_MODEL), lambda i: (jnp.maximum(i - lat_blocks, 0), 0))]


def _inproj(xs, mods, g, w, w_rot, bg, rope_c, rope_s, layer):
    lat_blocks = N_LAT // TM
    seq_blocks = SEQ // TM
    split = len(xs) == 2

    def rope_idx(i):
        return (jnp.where(i < lat_blocks, i % seq_blocks, seq_blocks), 0)

    widths = (512, 512, 512, 512, 128, 128, 512, 512, 3072)
    return pl.pallas_call(
        functools.partial(_inproj_kernel, split=split),
        grid=(N_TOK // TM,),
        in_specs=_token_specs(split) + [
            _mod_spec(0), _mod_spec(1),
            _const_spec((1, D_MODEL)),
            _layer_spec((D_MODEL, N_IN), layer),
            _layer_spec((D_MODEL, N_ROT), layer),
            _const_spec((1, 3 * D_MODEL)),
            pl.BlockSpec((TM, LANES), rope_idx),
            pl.BlockSpec((TM, LANES), rope_idx),
        ],
        out_specs=[pl.BlockSpec((TM, wd), lambda i: (i, 0)) for wd in widths],
        out_shape=[jax.ShapeDtypeStruct((N_TOK, wd), BF16) for wd in widths],
        compiler_params=_cparams(1),
        name="inproj",
    )(*xs, mods, mods, g, w, w_rot, bg, rope_c, rope_s)


ATTN_SUB = 2
ATTN_SLOTS = 4
ATTN_LAG = 4
ATTN_C_LAG = 2


def _lane_iota():
    return lax.broadcasted_iota(jnp.int32, (1, LANES), 1)


def _pipeline_step(step, stage_a, stage_b, stage_c, stage_d):
    for parity in range(ATTN_SLOTS // ATTN_SUB):
        @pl.when(step % (ATTN_SLOTS // ATTN_SUB) == parity)
        def _step(parity=parity):
            for k in range(ATTN_SUB):
                n = ATTN_SUB * step + k
                slot = (ATTN_SUB * parity + k) % ATTN_SLOTS
                stage_d(slot, n, k)
                stage_c((slot + ATTN_SLOTS - ATTN_C_LAG) % ATTN_SLOTS, n, k)
                stage_b((slot + ATTN_SLOTS - 1) % ATTN_SLOTS, n, k)
                stage_a(slot, n, k)


def _init_stage_buffers(s_scr, m_scr, pv_scr):
    s_scr[...] = jnp.zeros_like(s_scr)
    m_scr[...] = jnp.zeros_like(m_scr)
    pv_scr[...] = jnp.ones_like(pv_scr)


def _scores(qs, k_all, bias):
    s = _dot_nt(qs, k_all)
    nb = bias.shape[1]
    return jnp.concatenate([s[:, :nb] + bias, s[:, nb:]], axis=1)


def _row_max(s, sink):
    m = jnp.max(s, axis=-1, keepdims=True)
    if sink is not None:
        m = jnp.maximum(m, sink)
    return jnp.broadcast_to(m, (s.shape[0], LANES))


def _exp_pv(s_ref, m, v_all):
    nk = v_all.shape[0]
    e = jnp.concatenate(
        [jnp.exp(s_ref[:, LANES * c:LANES * (c + 1)] - m) for c in range(nk // LANES)], axis=1).astype(BF16)
    return _dot(e, jnp.concatenate([v_all, jnp.ones_like(v_all)], axis=1))


def _attn_row_block(b, j, lat_steps, tq):
    lat = b * lat_steps + j
    ctx = N_LAT // tq + b * (CTX_LEN // tq) + (j - lat_steps)
    return jnp.where(j < lat_steps, lat, ctx)


def _batch_spec(rows, width):
    return pl.BlockSpec((rows, width), lambda b, n: (b, 0), pipeline_mode=pl.Buffered(1))


def _ctx_spec(width):
    return pl.BlockSpec((CTX_LEN, width), lambda b, n: (N_LAT // CTX_LEN + b, 0))


def _attn_call(kernel_fn, name, operands, lead_specs, kv_specs, kv_width, tq, lat_steps, n_blocks, scratch):
    width = NA_HEADS * HEAD_DIM
    n_out = N_LAT + (n_blocks - lat_steps) * tq * BATCH
    n_steps = n_blocks // ATTN_SUB
    lag_steps = ATTN_LAG // ATTN_SUB
    tq = tq * ATTN_SUB
    lat_steps = lat_steps // ATTN_SUB

    def q_idx(b, n):
        return (_attn_row_block(b, jnp.minimum(n, n_steps - 1), lat_steps, tq), 0)

    def o_idx(b, n):
        return (_attn_row_block(b, jnp.maximum(n - lag_steps, 0), lat_steps, tq), 0)

    return pl.pallas_call(
        functools.partial(kernel_fn, n_blocks=n_blocks),
        grid=(BATCH, n_steps + lag_steps),
        in_specs=lead_specs + [
            pl.BlockSpec((tq, width), q_idx),
            *kv_specs,
            _ctx_spec(kv_width), _ctx_spec(kv_width),
        ],
        out_specs=pl.BlockSpec((tq, width), o_idx),
        out_shape=jax.ShapeDtypeStruct((n_out, width), BF16),
        scratch_shapes=scratch,
        compiler_params=_cparams(2),
        name=name,
    )(*operands)


NA_PAIRS = NA_HEADS // 2
NA_NK = NA_TK + CTX_LEN
NA_UROWS = ATTN_SUB * NA_QROWS + NA_WIN_ROWS


def _na_union_row(step):
    return jnp.clip(ATTN_SUB * NA_QROWS * step - NA_WIN_ROWS // 2, 0, GRID_ROWS - NA_UROWS)


def _na_kernel(bt_ref, q_ref, k_ref, v_ref, kc_ref, vc_ref, o_ref,
               bias_scr, s_scr, m_scr, pv_scr, *, n_blocks):
    step = pl.program_id(1)
    low_half = _lane_iota() < HEAD_DIM

    def window_row(j):
        return jnp.where(j < NA_STEPS,
                         jnp.clip(NA_QROWS * j - NA_WIN_ROWS // 2, 0, GRID_ROWS - NA_KROWS), 0)

    def window_start(chain, lag):
        kr0 = window_row(jnp.clip(chain, 0, n_blocks - 1))
        union0 = _na_union_row(jnp.maximum(step - lag, 0))
        return pl.multiple_of(jnp.clip(kr0 - union0, 0, NA_UROWS - NA_KROWS) * GRID_W, GRID_W)

    @pl.when(step == 0)
    def _init():
        _init_stage_buffers(s_scr, m_scr, pv_scr)

    first = jnp.minimum(ATTN_SUB * step, n_blocks - ATTN_SUB)

    @pl.when((first <= 2) | (first >= NA_STEPS - 2))
    def _regen():
        for k in range(ATTN_SUB):
            jb = first + k
            kr0 = window_row(jb)
            for i in range(NA_QROWS):
                qr = NA_QROWS * jb + i
                r0 = jnp.clip(qr - NA_WIN_ROWS // 2, 0, GRID_ROWS - NA_WIN_ROWS)
                for jj in range(0, NA_KROWS, 2):
                    tiles = []
                    for kk in (jj, jj + 1):
                        kr = kr0 + kk
                        valid = (kr >= r0) & (kr < r0 + NA_WIN_ROWS) & (jb < NA_STEPS)
                        idx = jnp.clip(kr - qr + NA_WIN_ROWS - 1, 0, 2 * NA_WIN_ROWS - 2)
                        tiles.append(jnp.where(valid, bt_ref[idx], NEG))
                    bias_scr[k, :, GRID_W * i:GRID_W * (i + 1), GRID_W * jj:GRID_W * (jj + 2)] = (
                        jnp.concatenate(tiles, axis=-1))

    def lanes(p):
        return slice(LANES * p, LANES * (p + 1))

    def sub_rows(k):
        return slice(NA_TQ * k, NA_TQ * (k + 1))

    def stage_a(slot, n, k):
        ks = window_start(n, 0)
        for p in range(NA_PAIRS):
            qp = q_ref[sub_rows(k), lanes(p)]
            qs = jnp.concatenate([jnp.where(low_half, qp, jnp.zeros_like(qp)),
                                  jnp.where(low_half, jnp.zeros_like(qp), qp)], axis=0)
            k_all = jnp.concatenate([k_ref[pl.ds(ks, NA_TK), lanes(p)], kc_ref[:, lanes(p)]], axis=0)
            s_scr[slot, p] = _scores(qs, k_all, bias_scr[k, 2 * p:2 * p + 2].reshape(2 * NA_TQ, NA_TK))

    def stage_b(slot, n, k):
        for p in range(NA_PAIRS):
            m_scr[slot, p] = _row_max(s_scr[slot, p], None)

    def stage_c(slot, n, k):
        ks = window_start(n - ATTN_C_LAG, ATTN_C_LAG // ATTN_SUB)
        for p in range(NA_PAIRS):
            v_all = jnp.concatenate([v_ref[pl.ds(ks, NA_TK), lanes(p)], vc_ref[:, lanes(p)]], axis=0)
            pv_scr[slot, p] = _exp_pv(s_scr.at[slot, p], m_scr[slot, p], v_all)

    def stage_d(slot, n, k):
        for p in range(NA_PAIRS):
            pv = pv_scr[slot, p]
            out = pv[:, :LANES] / pv[:, LANES:]
            o_ref[sub_rows(k), lanes(p)] = jnp.where(low_half, out[:NA_TQ], out[NA_TQ:]).astype(o_ref.dtype)

    _pipeline_step(step, stage_a, stage_b, stage_c, stage_d)


def _na_attention(qa, ka, va, btab, with_ctx, layer):
    n_blocks = NA_STEPS + (NA_CTX_STEPS if with_ctx else 0)
    rows = 2 * NA_TQ
    scratch = [
        pltpu.VMEM((ATTN_SUB, NA_HEADS, NA_TQ, NA_TK), F32),
        pltpu.VMEM((ATTN_SLOTS, NA_PAIRS, rows, NA_NK), F32),
        pltpu.VMEM((ATTN_SLOTS, NA_PAIRS, rows, LANES), F32),
        pltpu.VMEM((ATTN_SLOTS, NA_PAIRS, rows, 2 * LANES), F32),
    ]
    bt_spec = [_layer_spec((2 * NA_WIN_ROWS - 1, NA_HEADS, GRID_W, GRID_W), layer)]
    width = NA_HEADS * HEAD_DIM

    def window_spec(lag):
        return pl.BlockSpec(
            (pl.Element(NA_UROWS * GRID_W), pl.Element(width)),
            lambda b, s: (pl.multiple_of(b * SEQ + _na_union_row(jnp.maximum(s - lag, 0)) * GRID_W, GRID_W), 0))

    return _attn_call(_na_kernel, "na_attn", (btab, qa, ka, va, ka, va), bt_spec,
                      [window_spec(0), window_spec(ATTN_C_LAG // ATTN_SUB)],
                      width, NA_TQ, NA_STEPS, n_blocks, scratch)


SW_NK = SW_TK + CTX_LEN
SW_KV = SW_HEADS // SW_GROUP


def _swa_kernel(sink_ref, q_ref, k_ref, v_ref, kc_ref, vc_ref, o_ref,
                mask_scr, s_scr, m_scr, pv_scr, *, n_blocks):
    step = pl.program_id(1)
    lane = _lane_iota()
    low_half = lane < HEAD_DIM
    group0 = (lane % HEAD_DIM) < HEAD_DIM // 2

    def window_row(j):
        return jnp.where(j < SW_STEPS, jnp.clip(j * SW_TQ - SW_TQ, 0, SEQ - SW_TK), 0)

    def window_start(chain):
        return pl.multiple_of(window_row(jnp.clip(chain, 0, n_blocks - 1)), SW_TQ)

    def sink_rows(g, width):
        return jnp.concatenate(
            [jnp.full((SW_TQ, width), sink_ref[p + SW_GROUP * g], F32) for p in range(SW_GROUP)], axis=0)

    def lanes(p):
        return slice(LANES * p, LANES * (p + 1))

    @pl.when(step == 0)
    def _init():
        _init_stage_buffers(s_scr, m_scr, pv_scr)

    first = jnp.minimum(ATTN_SUB * step, n_blocks - ATTN_SUB)

    @pl.when((first <= 2) | (first >= SW_STEPS - 2))
    def _regen():
        for k in range(ATTN_SUB):
            jb = first + k
            rel = ((window_row(jb) - jb * SW_TQ) + lax.broadcasted_iota(jnp.int32, (SW_TQ, SW_TK), 1)
                   - lax.broadcasted_iota(jnp.int32, (SW_TQ, SW_TK), 0))
            band = jnp.where((jnp.abs(rel) <= SW_WINDOW) & (jb < SW_STEPS), 0.0, NEG)
            for p in range(SW_GROUP):
                mask_scr[k, SW_TQ * p:SW_TQ * (p + 1), :] = band

    def sub_rows(k):
        return slice(SW_TQ * k, SW_TQ * (k + 1))

    def stage_a(slot, n, k):
        ks = window_start(n)
        k_all = jnp.concatenate([k_ref[pl.ds(ks, SW_TK), :], kc_ref[...]], axis=0)
        for g in range(SW_KV):
            in_group = group0 if g == 0 else ~group0
            qs = jnp.concatenate(
                [jnp.where(in_group, q_ref[sub_rows(k), lanes(p)], jnp.zeros((SW_TQ, LANES), BF16))
                 for p in range(SW_GROUP)], axis=0)
            s_scr[slot, g] = _scores(qs, k_all, mask_scr[k])

    def stage_b(slot, n, k):
        for g in range(SW_KV):
            m_scr[slot, g] = _row_max(s_scr[slot, g], sink_rows(g, 1))

    def stage_c(slot, n, k):
        ks = window_start(n - ATTN_C_LAG)
        v_all = jnp.concatenate([v_ref[pl.ds(ks, SW_TK), :], vc_ref[...]], axis=0)
        for g in range(SW_KV):
            m = m_scr[slot, g]
            pv = _exp_pv(s_scr.at[slot, g], m, v_all)
            pv_scr[slot, g, :, :LANES] = pv[:, :LANES]
            pv_scr[slot, g, :, LANES:] = pv[:, LANES:] + jnp.exp(sink_rows(g, LANES) - m)

    def stage_d(slot, n, k):
        outs = [pv_scr[slot, g, :, :LANES] / pv_scr[slot, g, :, LANES:] for g in range(SW_KV)]
        for p in range(SW_GROUP):
            blk = slice(SW_TQ * p, SW_TQ * (p + 1))
            o_ref[sub_rows(k), lanes(p)] = jnp.where(low_half, outs[0][blk], outs[1][blk]).astype(o_ref.dtype)

    _pipeline_step(step, stage_a, stage_b, stage_c, stage_d)


def _swa_attention(sink, qb, kb, vb, with_ctx):
    n_blocks = SW_STEPS + (SW_CTX_STEPS if with_ctx else 0)
    rows = SW_GROUP * SW_TQ
    scratch = [
        pltpu.VMEM((ATTN_SUB, rows, SW_TK), F32),
        pltpu.VMEM((ATTN_SLOTS, SW_KV, rows, SW_NK), F32),
        pltpu.VMEM((ATTN_SLOTS, SW_KV, rows, LANES), F32),
        pltpu.VMEM((ATTN_SLOTS, SW_KV, rows, 2 * LANES), F32),
    ]
    return _attn_call(_swa_kernel, "swa_attn", (sink, qb, kb, vb, kb, vb),
                      [pl.BlockSpec(memory_space=pltpu.SMEM)], [_batch_spec(SEQ, LANES)] * 2,
                      LANES, SW_TQ, SW_STEPS, n_blocks, scratch)


HALO = 16


def _merge_kernel(*refs, split):
    x_ref, xc_ref = (refs[0], refs[1]) if split else (refs[0], None)
    (a_ref, b_ref, u_ref, up_ref, un_ref, cgb_ref, gate_ref, g1_ref, cw_ref,
     wa_ref, wb_ref, wc_ref, wo_ref, o_ref) = refs[2 if split else 1:]
    i = pl.program_id(0)
    local = lax.broadcasted_iota(jnp.int32, (MERGE_TM, 1), 0)
    row = i * MERGE_TM + local
    is_lat = row < N_LAT
    pos = jnp.where(is_lat, row & (SEQ - 1), (row - N_LAT) & (CTX_LEN - 1))
    seq_first = pos == 0
    seq_last = pos == jnp.where(is_lat, SEQ - 1, CTX_LEN - 1)

    u = u_ref[...].astype(F32)
    prev_row = up_ref[...].astype(F32)[HALO - 1:HALO, :]
    next_row = un_ref[...].astype(F32)[0:1, :]
    u_prev = jnp.where(local == 0, prev_row, pltpu.roll(u, 1, 0))
    u_prev = jnp.where(seq_first, 0.0, u_prev)
    u_next = jnp.where(local == MERGE_TM - 1, next_row, pltpu.roll(u, MERGE_TM - 1, 0))
    u_next = jnp.where(seq_last, 0.0, u_next)
    cw = cw_ref[...]
    y = cw[0:1, :] * u_prev + cw[1:2, :] * u + cw[2:3, :] * u_next
    c_out = (cgb_ref[...].astype(F32) * y).astype(BF16)

    merged = gate_ref[:, 0:D_MODEL].astype(F32) * _dot(a_ref[...], wa_ref[...])
    merged = merged + gate_ref[:, D_MODEL:2 * D_MODEL].astype(F32) * _dot(b_ref[...], wb_ref[...])
    merged = merged + gate_ref[:, 2 * D_MODEL:3 * D_MODEL].astype(F32) * _dot(c_out, wc_ref[...])
    o_ref[...] = _load_tokens(x_ref, xc_ref) + g1_ref[...] * _dot(merged.astype(BF16), wo_ref[...])


def _merge(xs, a, b, u, cgb, gates, mods, conv_w, wa, wb, wc, wo, n_rows, layer):
    halo_blocks = N_TOK // HALO
    per = MERGE_TM // HALO
    split = len(xs) == 2

    def tok(wd):
        return pl.BlockSpec((MERGE_TM, wd), lambda i: (i, 0))

    return pl.pallas_call(
        functools.partial(_merge_kernel, split=split),
        grid=(n_rows // MERGE_TM,),
        in_specs=_token_specs(split, MERGE_TM) + [
            tok(512), tok(512), tok(CONV_WIDTH),
            pl.BlockSpec((HALO, CONV_WIDTH), lambda i: (jnp.maximum(i * per - 1, 0), 0)),
            pl.BlockSpec((HALO, CONV_WIDTH), lambda i: (jnp.minimum((i + 1) * per, halo_blocks - 1), 0)),
            tok(CONV_WIDTH), tok(3 * D_MODEL),
            _mod_spec(2, MERGE_TM),
            _const_spec((3, CONV_WIDTH)),
            _layer_spec((512, D_MODEL), layer), _layer_spec((512, D_MODEL), layer),
            _layer_spec((512, D_MODEL), layer), _layer_spec((D_MODEL, D_MODEL), layer),
        ],
        out_specs=tok(D_MODEL),
        out_shape=jax.ShapeDtypeStruct((n_rows, D_MODEL), F32),
        compiler_params=_cparams(1),
        name="merge",
    )(*xs, a, b, u, u, u, cgb, gates, mods, conv_w, wa, wb, wc, wo)


FF_CHUNKS = tuple((c, min(c + 512, D_FF)) for c in range(0, D_FF, 512))


def _ffn_kernel(x_ref, sh_ref, sc_ref, g2_ref, ng_ref, fg_ref, wu_ref, wd_ref, o_ref, *, final):
    x = x_ref[...]
    h = _mod_norm(x, ng_ref[...], sh_ref[...], sc_ref[...]).astype(BF16)
    acc = None
    for c0, c1 in FF_CHUNKS:
        gate = _dot(h, wu_ref[:, c0:c1])
        up = _dot(h, wu_ref[:, D_FF + c0:D_FF + c1])
        act = (gate * jax.nn.sigmoid(gate) * up).astype(BF16)
        part = _dot(act, wd_ref[c0:c1, :])
        acc = part if acc is None else acc + part
    xn = x + g2_ref[...] * acc
    if final:
        xn = xn * lax.rsqrt(jnp.mean(xn * xn, axis=-1, keepdims=True) + NORM_EPS) * fg_ref[...]
    o_ref[...] = xn


def _ffn(xf, mods, ng, fg, wu, wd, final, layer):
    n_rows = xf.shape[0]
    return pl.pallas_call(
        functools.partial(_ffn_kernel, final=final),
        grid=(n_rows // FFN_TM,),
        in_specs=[
            pl.BlockSpec((FFN_TM, D_MODEL), lambda i: (i, 0)),
            _mod_spec(3, FFN_TM), _mod_spec(4, FFN_TM), _mod_spec(5, FFN_TM),
            _const_spec((1, D_MODEL)), _const_spec((1, D_MODEL)),
            _layer_spec((D_MODEL, 2 * D_FF), layer), _layer_spec((D_FF, D_MODEL), layer),
        ],
        out_specs=pl.BlockSpec((FFN_TM, D_MODEL), lambda i: (i, 0)),
        out_shape=jax.ShapeDtypeStruct((n_rows, D_MODEL), F32),
        compiler_params=_cparams(1),
        name="ffn",
    )(xf, mods, mods, mods, ng, fg, wu, wd)


def _prep_w_rot(w_in):
    half = HEAD_DIM // 2
    lead = w_in.shape[:-1]
    qb = w_in[..., C_QB:C_KB].reshape(*lead, 2, SW_GROUP, 2, half)
    qb = jnp.moveaxis(qb, -4, -2).reshape(*lead, SW_HEADS * HEAD_DIM)
    kb = w_in[..., C_KB:C_VB].reshape(*lead, 2, 2, half)
    kb = jnp.swapaxes(kb, -3, -2).reshape(*lead, 2 * HEAD_DIM)
    return jnp.concatenate([qb, kb, w_in[..., C_VB:C_CX]], axis=-1).astype(BF16)


def _prep_w_br_b(w_br_b):
    w = w_br_b.reshape(DEPTH, 2, SW_GROUP, HEAD_DIM, D_MODEL)
    return jnp.swapaxes(w, 1, 2).reshape(DEPTH, SW_HEADS * HEAD_DIM, D_MODEL).astype(BF16)


def _rope_tables():
    t = np.arange(SEQ)
    row = (t // GRID_W).astype(np.float32)
    col = (t % GRID_W).astype(np.float32)
    inv = jnp.asarray(ROPE_THETA, F32) ** (-jnp.arange(ROPE_FREQS, dtype=F32) / ROPE_FREQS)
    ang = jnp.concatenate([jnp.asarray(row)[:, None] * inv, jnp.asarray(col)[:, None] * inv], axis=-1)
    cos = jnp.cos(ang)
    sin = jnp.sin(ang)
    cos4 = jnp.concatenate([cos, cos, cos, cos], axis=-1)
    sin4 = jnp.concatenate([-sin, -sin, sin, sin], axis=-1)
    cos4 = jnp.concatenate([cos4, jnp.ones((TM, LANES), F32)], axis=0)
    sin4 = jnp.concatenate([sin4, jnp.zeros((TM, LANES), F32)], axis=0)
    return cos4, sin4


def _bias_tiles(rpb):
    qc = np.arange(GRID_W)[:, None]
    kc = np.arange(GRID_W)[None, :]
    start = np.clip(qc - NA_WIN_COLS // 2, 0, GRID_W - NA_WIN_COLS)
    valid = (kc >= start) & (kc < start + NA_WIN_COLS)
    dcol = kc - qc + NA_WIN_COLS - 1
    onehot = ((dcol[None] == np.arange(2 * NA_WIN_COLS - 1)[:, None, None]) & valid[None]).astype(np.float32)
    tiles = jnp.einsum("lhrd,dqk->lrhqk", rpb.astype(F32), jnp.asarray(onehot),
                       precision=lax.Precision.HIGHEST)
    return tiles + jnp.asarray(np.where(valid, 0.0, NEG).astype(np.float32))


def kernel(x, c, ctx, c_ctx, w_ada, b_ada, norm1_g, norm2_g, w_in, b_gate, na_rpb, sw_sink, conv_w,
           w_br_a, w_br_b, w_br_c, w_o, w_ffn_up, w_ffn_down, final_norm_g):
    xs = (x.reshape(N_LAT, D_MODEL), ctx.reshape(N_CTX, D_MODEL))
    cvecs = jnp.concatenate([c, c_ctx[None, :], jnp.zeros((8 - BATCH - 1, D_MODEL), F32)], axis=0)
    mods_all = _ada(cvecs, w_ada, b_ada).reshape(DEPTH, 8, 1, 6 * D_MODEL)

    rope_c, rope_s = _rope_tables()
    final_g = final_norm_g.reshape(1, D_MODEL)
    w_in_b, w_rot_b = w_in.astype(BF16), _prep_w_rot(w_in)
    w_a_b, w_b_b, w_c_b = w_br_a.astype(BF16), _prep_w_br_b(w_br_b), w_br_c.astype(BF16)
    w_o_b, w_up_b, w_dn_b = w_o.astype(BF16), w_ffn_up.astype(BF16), w_ffn_down.astype(BF16)
    btab = _bias_tiles(na_rpb)

    for l in range(DEPTH):
        last = l == DEPTH - 1
        mods = mods_all[l]
        qa, ka, va, qb, kb, vb, u, cgb, gates = _inproj(
            xs, mods, norm1_g[l].reshape(1, D_MODEL), w_in_b, w_rot_b,
            b_gate[l].reshape(1, 3 * D_MODEL), rope_c, rope_s, l)
        a_out = _na_attention(qa, ka, va, btab, not last, l)
        b_out = _swa_attention(sw_sink[l], qb, kb, vb, not last)
        n_rows = N_LAT if last else N_TOK
        xf = _merge(xs, a_out, b_out, u, cgb, gates, mods, conv_w[l],
                    w_a_b, w_b_b, w_c_b, w_o_b, n_rows, l)
        xf = _ffn(xf, mods, norm2_g[l].reshape(1, D_MODEL), final_g, w_up_b, w_dn_b, last, l)
        xs = (xf,)
    return xf.reshape(BATCH, SEQ, D_MODEL)
```

```python
import functools

import numpy as np
import jax
import jax.numpy as jnp
from jax import lax
from jax.experimental import pallas as pl
from jax.experimental.pallas import tpu as pltpu

F32 = jnp.float32
BF16 = jnp.bfloat16

D_MODEL = 1024
BATCH = 4
SEQ = 8192
DEPTH = 4
GRID_W = 64
GRID_ROWS = SEQ // GRID_W
CTX_LEN = 256
HEAD_DIM = 64
NORM_EPS = 1e-6
NA_HEADS = 8
NA_WIN_ROWS = 8
NA_WIN_COLS = 16
SW_HEADS = 8
SW_WINDOW = 128
ROPE_THETA = 10000.0
ROPE_FREQS = HEAD_DIM // 4
CONV_WIDTH = 512
D_FF = 2816
N_IN = 6912
N_ROT = 768

N_LAT = BATCH * SEQ
N_CTX = BATCH * CTX_LEN
N_TOK = N_LAT + N_CTX

LANES = 128
NEG = -1e30

C_QA, C_KA, C_VA, C_QB, C_KB, C_VB, C_CX, C_CGB, C_CGC, C_ZG = (
    0, 512, 1024, 1536, 2048, 2176, 2304, 2816, 3328, 3840)

TM = 512
MERGE_TM = 1024
FFN_TM = 1024
NA_QROWS = 2
NA_KROWS = NA_QROWS + NA_WIN_ROWS
NA_TQ = NA_QROWS * GRID_W
NA_TK = NA_KROWS * GRID_W
NA_STEPS = GRID_ROWS // NA_QROWS
SW_TQ = 128
SW_TK = 3 * SW_TQ
SW_STEPS = SEQ // SW_TQ
SW_GROUP = 4
VMEM_LIMIT = 56 * 1024 * 1024


def _cparams(n_axes):
    return pltpu.CompilerParams(dimension_semantics=("arbitrary",) * n_axes,
                                vmem_limit_bytes=VMEM_LIMIT)


def _const_spec(shape):
    nd = len(shape)
    return pl.BlockSpec(shape, lambda *_: (0,) * nd, pipeline_mode=pl.Buffered(1))


def _layer_spec(shape, layer):
    nd = len(shape)
    return pl.BlockSpec((None,) + tuple(shape), lambda *_: (layer,) + (0,) * nd, pipeline_mode=pl.Buffered(1))


def _dot(a, b):
    return jnp.dot(a, b, preferred_element_type=F32)


def _dot_nt(a, b):
    return lax.dot_general(a, b, (((1,), (1,)), ((), ())), preferred_element_type=F32)


def _mod_norm(x, g, shift, scale):
    y = x * lax.rsqrt(jnp.mean(x * x, axis=-1, keepdims=True) + NORM_EPS) * g
    return y * (1.0 + scale) + shift


def _ada_kernel(c_ref, w_ref, b_ref, o_ref):
    c = c_ref[...]
    sc = c * jax.nn.sigmoid(c)
    o_ref[...] = jnp.dot(sc, w_ref[...], preferred_element_type=F32,
                         precision=lax.Precision.HIGHEST) + b_ref[...]


def _ada(cvecs, w_ada, b_ada):
    tn = 1536
    return pl.pallas_call(
        _ada_kernel,
        grid=(DEPTH, 6 * D_MODEL // tn),
        in_specs=[
            pl.BlockSpec((8, D_MODEL), lambda l, n: (0, 0)),
            pl.BlockSpec((None, D_MODEL, tn), lambda l, n: (l, 0, n)),
            pl.BlockSpec((None, 1, tn), lambda l, n: (l, 0, n)),
        ],
        out_specs=pl.BlockSpec((None, 8, tn), lambda l, n: (l, 0, n)),
        out_shape=jax.ShapeDtypeStruct((DEPTH, 8, 6 * D_MODEL), F32),
        compiler_params=_cparams(2),
        name="ada",
    )(cvecs, w_ada, b_ada.reshape(DEPTH, 1, 6 * D_MODEL))


def _mod_row(i, tm):
    lat_blocks = N_LAT // tm
    return jnp.where(i < lat_blocks, i // (SEQ // tm), BATCH)


def _mod_spec(part, tm=TM):
    return pl.BlockSpec((None, 1, D_MODEL), lambda i: (_mod_row(i, tm), 0, part))


def _load_tokens(x_ref, xc_ref):
    if xc_ref is None:
        return x_ref[...]
    return jnp.where(pl.program_id(0) < N_LAT // x_ref.shape[0], x_ref[...], xc_ref[...])


def _load_parts(parts):
    return _load_tokens(parts[0], parts[1] if len(parts) == 2 else None)


def _inproj_kernel(*refs, split):
    x_ref, xc_ref = (refs[0], refs[1]) if split else (refs[0], None)
    (sh_ref, sc_ref, g_ref, w_ref, wr_ref, bg_ref, cos_ref, sin_ref,
     qa_ref, ka_ref, va_ref, qb_ref, kb_ref, vb_ref, u_ref, cgb_ref, gate_ref) = refs[2 if split else 1:]
    h = _mod_norm(_load_tokens(x_ref, xc_ref), g_ref[...], sh_ref[...], sc_ref[...]).astype(BF16)
    q_scale = HEAD_DIM ** -0.5

    def proj(c0, c1):
        return _dot(h, w_ref[:, c0:c1])

    for k in range(3):
        z = proj(C_ZG + D_MODEL * k, C_ZG + D_MODEL * (k + 1))
        gate_ref[:, D_MODEL * k:D_MODEL * (k + 1)] = jax.nn.sigmoid(
            z + bg_ref[:, D_MODEL * k:D_MODEL * (k + 1)]).astype(BF16)

    cosv = cos_ref[...]
    sinv = sin_ref[...]

    def rope(t):
        return t * cosv + pltpu.roll(t, LANES // 2, 1) * sinv

    rot = _dot(h, wr_ref[...])
    for p in range(4):
        qb_ref[:, LANES * p:LANES * (p + 1)] = (
            rope(rot[:, LANES * p:LANES * (p + 1)]) * q_scale).astype(BF16)
    kb_ref[...] = rope(rot[:, 4 * LANES:5 * LANES]).astype(BF16)
    vb_ref[...] = rot[:, 5 * LANES:].astype(BF16)

    u_ref[...] = (proj(C_CGC, C_ZG) * proj(C_CX, C_CGB)).astype(BF16)
    qa_ref[...] = (proj(C_QA, C_KA) * q_scale).astype(BF16)
    cgb_ref[...] = proj(C_CGB, C_CGC).astype(BF16)
    ka_ref[...] = proj(C_KA, C_VA).astype(BF16)
    va_ref[...] = proj(C_VA, C_QB).astype(BF16)


def _token_specs(split, tm=TM, width=D_MODEL):
    lat_blocks = N_LAT // tm
    if not split:
        return [pl.BlockSpec((tm, width), lambda i: (i, 0))]
    return [pl.BlockSpec((tm, width), lambda i: (jnp.minimum(i, lat_blocks - 1), 0)),
            pl.BlockSpec((tm, width), lambda i: (jnp.maximum(i - lat_blocks, 0), 0),
                         pipeline_mode=pl.Buffered(1))]


def _inproj(xs, mods, g, w, w_rot, bg, rope_c, rope_s, layer):
    lat_blocks = N_LAT // TM
    seq_blocks = SEQ // TM
    split = len(xs) == 2

    def rope_idx(i):
        return (jnp.where(i < lat_blocks, i % seq_blocks, seq_blocks), 0)

    widths = (512, 512, 512, 512, 128, 128, 512, 512, 3072)
    return pl.pallas_call(
        functools.partial(_inproj_kernel, split=split),
        grid=(N_TOK // TM,),
        in_specs=_token_specs(split) + [
            _mod_spec(0), _mod_spec(1),
            _const_spec((1, D_MODEL)),
            _layer_spec((D_MODEL, N_IN), layer),
            _layer_spec((D_MODEL, N_ROT), layer),
            _const_spec((1, 3 * D_MODEL)),
            pl.BlockSpec((TM, LANES), rope_idx),
            pl.BlockSpec((TM, LANES), rope_idx),
        ],
        out_specs=[pl.BlockSpec((TM, wd), lambda i: (i, 0)) for wd in widths],
        out_shape=[jax.ShapeDtypeStruct((N_TOK, wd), BF16) for wd in widths],
        compiler_params=_cparams(1),
        name="inproj",
    )(*xs, mods, mods, g, w, w_rot, bg, rope_c, rope_s)


ATTN_SUB = 4
ATTN_SLOTS = 4
ATTN_LAG = 4
ATTN_C_LAG = 2


def _lane_iota():
    return lax.broadcasted_iota(jnp.int32, (1, LANES), 1)


def _pipeline_step(step, stage_a, stage_b, stage_c, stage_d):
    assert ATTN_SUB == ATTN_SLOTS and ATTN_LAG == ATTN_SUB
    for k in range(ATTN_SUB):
        n = ATTN_SUB * step + k
        stage_d(k, n, k)
        stage_c((k - ATTN_C_LAG) % ATTN_SLOTS, n, k)
        stage_b((k - 1) % ATTN_SLOTS, n, k)
        stage_a(k, n, k)


def _init_stage_buffers(s_scr, m_scr, pv_scr):
    s_scr[...] = jnp.zeros_like(s_scr)
    m_scr[...] = jnp.zeros_like(m_scr)
    pv_scr[...] = jnp.ones_like(pv_scr)


def _scores(qs, k_all, bias):
    s = _dot_nt(qs, k_all)
    nb = bias.shape[1]
    return jnp.concatenate([s[:, :nb] + bias, s[:, nb:]], axis=1)


def _row_max(s, sink):
    m = jnp.max(s, axis=-1, keepdims=True)
    if sink is not None:
        m = jnp.maximum(m, sink)
    return jnp.broadcast_to(m, (s.shape[0], LANES))


def _exp_pv(s_ref, m, v_all):
    nk = v_all.shape[0]
    e = jnp.concatenate(
        [jnp.exp(s_ref[:, LANES * c:LANES * (c + 1)] - m) for c in range(nk // LANES)], axis=1).astype(BF16)
    return _dot(e, jnp.concatenate([v_all, jnp.ones_like(v_all)], axis=1))


def _batch_spec(rows, width):
    return pl.BlockSpec((rows, width), lambda b, n: (b, 0), pipeline_mode=pl.Buffered(1))


def _ctx_spec(width):
    return pl.BlockSpec((CTX_LEN, width), lambda b, n: (N_LAT // CTX_LEN + b, 0))


def _attn_call(kernel_fn, name, operands, lead_specs, kv_specs, kv_width, tq, n_blocks, scratch):
    width = NA_HEADS * HEAD_DIM
    n_out = N_LAT
    n_steps = n_blocks // ATTN_SUB
    lag_steps = ATTN_LAG // ATTN_SUB
    tq = tq * ATTN_SUB

    def q_idx(b, n):
        return (b * n_steps + jnp.minimum(n, n_steps - 1), 0)

    def o_idx(b, n):
        return (b * n_steps + jnp.maximum(n - lag_steps, 0), 0)

    return pl.pallas_call(
        functools.partial(kernel_fn, n_blocks=n_blocks),
        grid=(BATCH, n_steps + lag_steps),
        in_specs=lead_specs + [
            pl.BlockSpec((tq, width), q_idx),
            *kv_specs,
            _ctx_spec(kv_width), _ctx_spec(kv_width),
        ],
        out_specs=pl.BlockSpec((tq, width), o_idx),
        out_shape=jax.ShapeDtypeStruct((n_out, width), BF16),
        scratch_shapes=scratch,
        compiler_params=_cparams(2),
        name=name,
    )(*operands)


NA_PAIRS = NA_HEADS // 2
NA_NK = NA_TK + CTX_LEN
NA_UROWS = ATTN_SUB * NA_QROWS + NA_WIN_ROWS


def _na_union_row(step, block_lag):
    first_qrow = NA_QROWS * (ATTN_SUB * step - block_lag)
    return jnp.clip(first_qrow - NA_WIN_ROWS // 2, 0, GRID_ROWS - NA_UROWS)


def _na_kernel(bt_ref, q_ref, k_ref, v_ref, kc_ref, vc_ref, o_ref,
               bias_scr, s_scr, m_scr, pv_scr, *, n_blocks):
    step = pl.program_id(1)
    low_half = _lane_iota() < HEAD_DIM

    def window_row(j):
        return jnp.clip(NA_QROWS * j - NA_WIN_ROWS // 2, 0, GRID_ROWS - NA_KROWS)

    def window_start(chain, block_lag):
        kr0 = window_row(jnp.clip(chain, 0, n_blocks - 1))
        union0 = _na_union_row(step, block_lag)
        return pl.multiple_of(jnp.clip(kr0 - union0, 0, NA_UROWS - NA_KROWS) * GRID_W, GRID_W)

    @pl.when(step == 0)
    def _init():
        _init_stage_buffers(s_scr, m_scr, pv_scr)

    first = jnp.minimum(ATTN_SUB * step, n_blocks - ATTN_SUB)

    @pl.when((first <= ATTN_SUB) | (first >= NA_STEPS - ATTN_SUB))
    def _regen():
        for k in range(ATTN_SUB):
            jb = first + k
            kr0 = window_row(jb)
            for i in range(NA_QROWS):
                qr = NA_QROWS * jb + i
                r0 = jnp.clip(qr - NA_WIN_ROWS // 2, 0, GRID_ROWS - NA_WIN_ROWS)
                for jj in range(0, NA_KROWS, 2):
                    tiles = []
                    for kk in (jj, jj + 1):
                        kr = kr0 + kk
                        valid = (kr >= r0) & (kr < r0 + NA_WIN_ROWS)
                        idx = jnp.clip(kr - qr + NA_WIN_ROWS - 1, 0, 2 * NA_WIN_ROWS - 2)
                        tiles.append(jnp.where(valid, bt_ref[idx], NEG))
                    bias_scr[k, :, GRID_W * i:GRID_W * (i + 1), GRID_W * jj:GRID_W * (jj + 2)] = (
                        jnp.concatenate(tiles, axis=-1))

    def lanes(p):
        return slice(LANES * p, LANES * (p + 1))

    def sub_rows(k):
        return slice(NA_TQ * k, NA_TQ * (k + 1))

    def stage_a(slot, n, k):
        ks = window_start(n, 0)
        for p in range(NA_PAIRS):
            qp = q_ref[sub_rows(k), lanes(p)]
            qs = jnp.concatenate([jnp.where(low_half, qp, jnp.zeros_like(qp)),
                                  jnp.where(low_half, jnp.zeros_like(qp), qp)], axis=0)
            k_all = jnp.concatenate([k_ref[pl.ds(ks, NA_TK), lanes(p)], kc_ref[:, lanes(p)]], axis=0)
            s_scr[slot, p] = _scores(qs, k_all, bias_scr[k, 2 * p:2 * p + 2].reshape(2 * NA_TQ, NA_TK))

    def stage_b(slot, n, k):
        for p in range(NA_PAIRS):
            m_scr[slot, p] = _row_max(s_scr[slot, p], None)

    def stage_c(slot, n, k):
        ks = window_start(n - ATTN_C_LAG, ATTN_C_LAG)
        for p in range(NA_PAIRS):
            v_all = jnp.concatenate([v_ref[pl.ds(ks, NA_TK), lanes(p)], vc_ref[:, lanes(p)]], axis=0)
            pv_scr[slot, p] = _exp_pv(s_scr.at[slot, p], m_scr[slot, p], v_all)

    def stage_d(slot, n, k):
        for p in range(NA_PAIRS):
            pv = pv_scr[slot, p]
            out = pv[:, :LANES] / pv[:, LANES:]
            o_ref[sub_rows(k), lanes(p)] = jnp.where(low_half, out[:NA_TQ], out[NA_TQ:]).astype(o_ref.dtype)

    _pipeline_step(step, stage_a, stage_b, stage_c, stage_d)


def _na_attention(qa, ka, va, btab, layer):
    n_blocks = NA_STEPS
    rows = 2 * NA_TQ
    scratch = [
        pltpu.VMEM((ATTN_SUB, NA_HEADS, NA_TQ, NA_TK), F32),
        pltpu.VMEM((ATTN_SLOTS, NA_PAIRS, rows, NA_NK), F32),
        pltpu.VMEM((ATTN_SLOTS, NA_PAIRS, rows, LANES), F32),
        pltpu.VMEM((ATTN_SLOTS, NA_PAIRS, rows, 2 * LANES), F32),
    ]
    bt_spec = [_layer_spec((2 * NA_WIN_ROWS - 1, NA_HEADS, GRID_W, GRID_W), layer)]
    width = NA_HEADS * HEAD_DIM

    def window_spec(block_lag):
        return pl.BlockSpec(
            (pl.Element(NA_UROWS * GRID_W), pl.Element(width)),
            lambda b, s: (pl.multiple_of(b * SEQ + _na_union_row(s, block_lag) * GRID_W, GRID_W), 0))

    return _attn_call(_na_kernel, "na_attn", (btab, qa, ka, va, ka, va), bt_spec,
                      [window_spec(0), window_spec(ATTN_C_LAG)], width, NA_TQ, n_blocks, scratch)


SW_NK = SW_TK + CTX_LEN
SW_KV = SW_HEADS // SW_GROUP


def _swa_kernel(sink_ref, q_ref, k_ref, v_ref, kc_ref, vc_ref, o_ref,
                mask_scr, s_scr, m_scr, pv_scr, *, n_blocks):
    step = pl.program_id(1)
    lane = _lane_iota()
    low_half = lane < HEAD_DIM
    group0 = (lane % HEAD_DIM) < HEAD_DIM // 2

    def window_row(j):
        return jnp.clip(j * SW_TQ - SW_TQ, 0, SEQ - SW_TK)

    def window_start(chain):
        return pl.multiple_of(window_row(jnp.clip(chain, 0, n_blocks - 1)), SW_TQ)

    def sink_rows(g, width):
        return jnp.concatenate(
            [jnp.full((SW_TQ, width), sink_ref[p + SW_GROUP * g], F32) for p in range(SW_GROUP)], axis=0)

    def lanes(p):
        return slice(LANES * p, LANES * (p + 1))

    @pl.when(step == 0)
    def _init():
        _init_stage_buffers(s_scr, m_scr, pv_scr)

    first = jnp.minimum(ATTN_SUB * step, n_blocks - ATTN_SUB)

    @pl.when((first <= ATTN_SUB) | (first >= SW_STEPS - ATTN_SUB))
    def _regen():
        for k in range(ATTN_SUB):
            jb = first + k
            rel = ((window_row(jb) - jb * SW_TQ) + lax.broadcasted_iota(jnp.int32, (SW_TQ, SW_TK), 1)
                   - lax.broadcasted_iota(jnp.int32, (SW_TQ, SW_TK), 0))
            band = jnp.where(jnp.abs(rel) <= SW_WINDOW, 0.0, NEG)
            for p in range(SW_GROUP):
                mask_scr[k, SW_TQ * p:SW_TQ * (p + 1), :] = band

    def sub_rows(k):
        return slice(SW_TQ * k, SW_TQ * (k + 1))

    def stage_a(slot, n, k):
        ks = window_start(n)
        k_all = jnp.concatenate([k_ref[pl.ds(ks, SW_TK), :], kc_ref[...]], axis=0)
        for g in range(SW_KV):
            in_group = group0 if g == 0 else ~group0
            qs = jnp.concatenate(
                [jnp.where(in_group, q_ref[sub_rows(k), lanes(p)], jnp.zeros((SW_TQ, LANES), BF16))
                 for p in range(SW_GROUP)], axis=0)
            s_scr[slot, g] = _scores(qs, k_all, mask_scr[k])

    def stage_b(slot, n, k):
        for g in range(SW_KV):
            m_scr[slot, g] = _row_max(s_scr[slot, g], sink_rows(g, 1))

    def stage_c(slot, n, k):
        ks = window_start(n - ATTN_C_LAG)
        v_all = jnp.concatenate([v_ref[pl.ds(ks, SW_TK), :], vc_ref[...]], axis=0)
        for g in range(SW_KV):
            m = m_scr[slot, g]
            pv = _exp_pv(s_scr.at[slot, g], m, v_all)
            pv_scr[slot, g, :, :LANES] = pv[:, :LANES]
            pv_scr[slot, g, :, LANES:] = pv[:, LANES:] + jnp.exp(sink_rows(g, LANES) - m)

    def stage_d(slot, n, k):
        outs = [pv_scr[slot, g, :, :LANES] / pv_scr[slot, g, :, LANES:] for g in range(SW_KV)]
        for p in range(SW_GROUP):
            blk = slice(SW_TQ * p, SW_TQ * (p + 1))
            o_ref[sub_rows(k), lanes(p)] = jnp.where(low_half, outs[0][blk], outs[1][blk]).astype(o_ref.dtype)

    _pipeline_step(step, stage_a, stage_b, stage_c, stage_d)


def _swa_attention(sink, qb, kb, vb):
    n_blocks = SW_STEPS
    rows = SW_GROUP * SW_TQ
    scratch = [
        pltpu.VMEM((ATTN_SUB, rows, SW_TK), F32),
        pltpu.VMEM((ATTN_SLOTS, SW_KV, rows, SW_NK), F32),
        pltpu.VMEM((ATTN_SLOTS, SW_KV, rows, LANES), F32),
        pltpu.VMEM((ATTN_SLOTS, SW_KV, rows, 2 * LANES), F32),
    ]
    return _attn_call(_swa_kernel, "swa_attn", (sink, qb, kb, vb, kb, vb),
                      [pl.BlockSpec(memory_space=pltpu.SMEM)], [_batch_spec(SEQ, LANES)] * 2,
                      LANES, SW_TQ, n_blocks, scratch)


def _ctx_attn_kernel(sink_ref, qa_ref, ka_ref, va_ref, qb_ref, kb_ref, vb_ref, oa_ref, ob_ref):
    lane = _lane_iota()
    low_half = lane < HEAD_DIM
    group0 = (lane % HEAD_DIM) < HEAD_DIM // 2

    def lanes(p):
        return slice(LANES * p, LANES * (p + 1))

    def attend(qs, k, v, sink):
        s = _dot_nt(qs, k)
        m = _row_max(s, sink)
        e = jnp.concatenate([jnp.exp(s[:, LANES * c:LANES * (c + 1)] - m) for c in range(CTX_LEN // LANES)],
                            axis=1).astype(BF16)
        pv = _dot(e, jnp.concatenate([v, jnp.ones_like(v)], axis=1))
        den = pv[:, LANES:]
        if sink is not None:
            den = den + jnp.exp(sink - m)
        return pv[:, :LANES] / den

    for p in range(NA_PAIRS):
        qp = qa_ref[:, lanes(p)]
        qs = jnp.concatenate([jnp.where(low_half, qp, jnp.zeros_like(qp)),
                              jnp.where(low_half, jnp.zeros_like(qp), qp)], axis=0)
        out = attend(qs, ka_ref[:, lanes(p)], va_ref[:, lanes(p)], None)
        oa_ref[:, lanes(p)] = jnp.where(low_half, out[:CTX_LEN], out[CTX_LEN:]).astype(oa_ref.dtype)

    outs = []
    for g in range(SW_KV):
        in_group = group0 if g == 0 else ~group0
        qs = jnp.concatenate(
            [jnp.where(in_group, qb_ref[:, lanes(p)], jnp.zeros((CTX_LEN, LANES), BF16))
             for p in range(SW_GROUP)], axis=0)
        sink = jnp.concatenate(
            [jnp.full((CTX_LEN, 1), sink_ref[p + SW_GROUP * g], F32) for p in range(SW_GROUP)], axis=0)
        outs.append(attend(qs, kb_ref[...], vb_ref[...], sink))
    for p in range(SW_GROUP):
        blk = slice(CTX_LEN * p, CTX_LEN * (p + 1))
        ob_ref[:, lanes(p)] = jnp.where(low_half, outs[0][blk], outs[1][blk]).astype(ob_ref.dtype)


def _ctx_attention(sink, qa, ka, va, qb, kb, vb):
    width = NA_HEADS * HEAD_DIM
    ctx0 = N_LAT // CTX_LEN

    def ctx_rows(w):
        return pl.BlockSpec((CTX_LEN, w), lambda b: (ctx0 + b, 0))

    return pl.pallas_call(
        _ctx_attn_kernel,
        grid=(BATCH,),
        in_specs=[pl.BlockSpec(memory_space=pltpu.SMEM),
                  ctx_rows(width), ctx_rows(width), ctx_rows(width),
                  ctx_rows(width), ctx_rows(LANES), ctx_rows(LANES)],
        out_specs=[pl.BlockSpec((CTX_LEN, width), lambda b: (b, 0))] * 2,
        out_shape=[jax.ShapeDtypeStruct((N_CTX, width), BF16)] * 2,
        compiler_params=_cparams(1),
        name="ctx_attn",
    )(sink, qa, ka, va, qb, kb, vb)


HALO = 16


def _merge_kernel(*refs, n_x, n_ab):
    x_parts, a_parts, b_parts = refs[:n_x], refs[n_x:n_x + n_ab], refs[n_x + n_ab:n_x + 2 * n_ab]
    (u_ref, up_ref, un_ref, cgb_ref, gate_ref, g1_ref, cw_ref,
     wa_ref, wb_ref, wc_ref, wo_ref, o_ref) = refs[n_x + 2 * n_ab:]
    i = pl.program_id(0)
    local = lax.broadcasted_iota(jnp.int32, (MERGE_TM, 1), 0)
    row = i * MERGE_TM + local
    is_lat = row < N_LAT
    pos = jnp.where(is_lat, row & (SEQ - 1), (row - N_LAT) & (CTX_LEN - 1))
    seq_first = pos == 0
    seq_last = pos == jnp.where(is_lat, SEQ - 1, CTX_LEN - 1)

    u = u_ref[...].astype(F32)
    prev_row = up_ref[...].astype(F32)[HALO - 1:HALO, :]
    next_row = un_ref[...].astype(F32)[0:1, :]
    u_prev = jnp.where(local == 0, prev_row, pltpu.roll(u, 1, 0))
    u_prev = jnp.where(seq_first, 0.0, u_prev)
    u_next = jnp.where(local == MERGE_TM - 1, next_row, pltpu.roll(u, MERGE_TM - 1, 0))
    u_next = jnp.where(seq_last, 0.0, u_next)
    cw = cw_ref[...]
    y = cw[0:1, :] * u_prev + cw[1:2, :] * u + cw[2:3, :] * u_next
    c_out = (cgb_ref[...].astype(F32) * y).astype(BF16)

    merged = gate_ref[:, 0:D_MODEL].astype(F32) * _dot(_load_parts(a_parts), wa_ref[...])
    merged = merged + gate_ref[:, D_MODEL:2 * D_MODEL].astype(F32) * _dot(_load_parts(b_parts), wb_ref[...])
    merged = merged + gate_ref[:, 2 * D_MODEL:3 * D_MODEL].astype(F32) * _dot(c_out, wc_ref[...])
    o_ref[...] = _load_parts(x_parts) + g1_ref[...] * _dot(merged.astype(BF16), wo_ref[...])


def _merge(xs, a_parts, b_parts, u, cgb, gates, mods, conv_w, wa, wb, wc, wo, n_rows, layer):
    halo_blocks = N_TOK // HALO
    per = MERGE_TM // HALO
    width = NA_HEADS * HEAD_DIM
    ab_specs = _token_specs(len(a_parts) == 2, MERGE_TM, width)

    def tok(wd):
        return pl.BlockSpec((MERGE_TM, wd), lambda i: (i, 0))

    return pl.pallas_call(
        functools.partial(_merge_kernel, n_x=len(xs), n_ab=len(a_parts)),
        grid=(n_rows // MERGE_TM,),
        in_specs=_token_specs(len(xs) == 2, MERGE_TM) + ab_specs + ab_specs + [
            tok(CONV_WIDTH),
            pl.BlockSpec((HALO, CONV_WIDTH), lambda i: (jnp.maximum(i * per - 1, 0), 0)),
            pl.BlockSpec((HALO, CONV_WIDTH), lambda i: (jnp.minimum((i + 1) * per, halo_blocks - 1), 0)),
            tok(CONV_WIDTH), tok(3 * D_MODEL),
            _mod_spec(2, MERGE_TM),
            _const_spec((3, CONV_WIDTH)),
            _layer_spec((512, D_MODEL), layer), _layer_spec((512, D_MODEL), layer),
            _layer_spec((512, D_MODEL), layer), _layer_spec((D_MODEL, D_MODEL), layer),
        ],
        out_specs=tok(D_MODEL),
        out_shape=jax.ShapeDtypeStruct((n_rows, D_MODEL), F32),
        compiler_params=_cparams(1),
        name="merge",
    )(*xs, *a_parts, *b_parts, u, u, u, cgb, gates, mods, conv_w, wa, wb, wc, wo)


FF_CHUNKS = tuple((c, min(c + 512, D_FF)) for c in range(0, D_FF, 512))


def _ffn_kernel(x_ref, sh_ref, sc_ref, g2_ref, ng_ref, fg_ref, wu_ref, wd_ref, o_ref, *, final):
    x = x_ref[...]
    h = _mod_norm(x, ng_ref[...], sh_ref[...], sc_ref[...]).astype(BF16)
    acc = None
    for c0, c1 in FF_CHUNKS:
        gate = _dot(h, wu_ref[:, c0:c1])
        up = _dot(h, wu_ref[:, D_FF + c0:D_FF + c1])
        act = (gate * jax.nn.sigmoid(gate) * up).astype(BF16)
        part = _dot(act, wd_ref[c0:c1, :])
        acc = part if acc is None else acc + part
    xn = x + g2_ref[...] * acc
    if final:
        xn = xn * lax.rsqrt(jnp.mean(xn * xn, axis=-1, keepdims=True) + NORM_EPS) * fg_ref[...]
    o_ref[...] = xn


def _ffn(xf, mods, ng, fg, wu, wd, final, layer):
    n_rows = xf.shape[0]
    return pl.pallas_call(
        functools.partial(_ffn_kernel, final=final),
        grid=(n_rows // FFN_TM,),
        in_specs=[
            pl.BlockSpec((FFN_TM, D_MODEL), lambda i: (i, 0)),
            _mod_spec(3, FFN_TM), _mod_spec(4, FFN_TM), _mod_spec(5, FFN_TM),
            _const_spec((1, D_MODEL)), _const_spec((1, D_MODEL)),
            _layer_spec((D_MODEL, 2 * D_FF), layer), _layer_spec((D_FF, D_MODEL), layer),
        ],
        out_specs=pl.BlockSpec((FFN_TM, D_MODEL), lambda i: (i, 0)),
        out_shape=jax.ShapeDtypeStruct((n_rows, D_MODEL), F32),
        compiler_params=_cparams(1),
        name="ffn",
    )(xf, mods, mods, mods, ng, fg, wu, wd)


def _prep_w_rot(w_in):
    half = HEAD_DIM // 2
    lead = w_in.shape[:-1]
    qb = w_in[..., C_QB:C_KB].reshape(*lead, 2, SW_GROUP, 2, half)
    qb = jnp.moveaxis(qb, -4, -2).reshape(*lead, SW_HEADS * HEAD_DIM)
    kb = w_in[..., C_KB:C_VB].reshape(*lead, 2, 2, half)
    kb = jnp.swapaxes(kb, -3, -2).reshape(*lead, 2 * HEAD_DIM)
    return jnp.concatenate([qb, kb, w_in[..., C_VB:C_CX]], axis=-1).astype(BF16)


def _prep_w_br_b(w_br_b):
    w = w_br_b.reshape(DEPTH, 2, SW_GROUP, HEAD_DIM, D_MODEL)
    return jnp.swapaxes(w, 1, 2).reshape(DEPTH, SW_HEADS * HEAD_DIM, D_MODEL).astype(BF16)


def _rope_tables():
    t = np.arange(SEQ)
    row = (t // GRID_W).astype(np.float32)
    col = (t % GRID_W).astype(np.float32)
    inv = jnp.asarray(ROPE_THETA, F32) ** (-jnp.arange(ROPE_FREQS, dtype=F32) / ROPE_FREQS)
    ang = jnp.concatenate([jnp.asarray(row)[:, None] * inv, jnp.asarray(col)[:, None] * inv], axis=-1)
    cos = jnp.cos(ang)
    sin = jnp.sin(ang)
    cos4 = jnp.concatenate([cos, cos, cos, cos], axis=-1)
    sin4 = jnp.concatenate([-sin, -sin, sin, sin], axis=-1)
    cos4 = jnp.concatenate([cos4, jnp.ones((TM, LANES), F32)], axis=0)
    sin4 = jnp.concatenate([sin4, jnp.zeros((TM, LANES), F32)], axis=0)
    return cos4, sin4


def _bias_tiles(rpb):
    qc = np.arange(GRID_W)[:, None]
    kc = np.arange(GRID_W)[None, :]
    start = np.clip(qc - NA_WIN_COLS // 2, 0, GRID_W - NA_WIN_COLS)
    valid = (kc >= start) & (kc < start + NA_WIN_COLS)
    dcol = kc - qc + NA_WIN_COLS - 1
    onehot = ((dcol[None] == np.arange(2 * NA_WIN_COLS - 1)[:, None, None]) & valid[None]).astype(np.float32)
    tiles = jnp.einsum("lhrd,dqk->lrhqk", rpb.astype(F32), jnp.asarray(onehot),
                       precision=lax.Precision.HIGHEST)
    return tiles + jnp.asarray(np.where(valid, 0.0, NEG).astype(np.float32))


def kernel(x, c, ctx, c_ctx, w_ada, b_ada, norm1_g, norm2_g, w_in, b_gate, na_rpb, sw_sink, conv_w,
           w_br_a, w_br_b, w_br_c, w_o, w_ffn_up, w_ffn_down, final_norm_g):
    xs = (x.reshape(N_LAT, D_MODEL), ctx.reshape(N_CTX, D_MODEL))
    cvecs = jnp.concatenate([c, c_ctx[None, :], jnp.zeros((8 - BATCH - 1, D_MODEL), F32)], axis=0)
    mods_all = _ada(cvecs, w_ada, b_ada).reshape(DEPTH, 8, 1, 6 * D_MODEL)

    rope_c, rope_s = _rope_tables()
    final_g = final_norm_g.reshape(1, D_MODEL)
    w_in_b, w_rot_b = w_in.astype(BF16), _prep_w_rot(w_in)
    w_a_b, w_b_b, w_c_b = w_br_a.astype(BF16), _prep_w_br_b(w_br_b), w_br_c.astype(BF16)
    w_o_b, w_up_b, w_dn_b = w_o.astype(BF16), w_ffn_up.astype(BF16), w_ffn_down.astype(BF16)
    btab = _bias_tiles(na_rpb)

    for l in range(DEPTH):
        last = l == DEPTH - 1
        mods = mods_all[l]
        qa, ka, va, qb, kb, vb, u, cgb, gates = _inproj(
            xs, mods, norm1_g[l].reshape(1, D_MODEL), w_in_b, w_rot_b,
            b_gate[l].reshape(1, 3 * D_MODEL), rope_c, rope_s, l)
        a_parts = (_na_attention(qa, ka, va, btab, l),)
        b_parts = (_swa_attention(sw_sink[l], qb, kb, vb),)
        if not last:
            a_ctx, b_ctx = _ctx_attention(sw_sink[l], qa, ka, va, qb, kb, vb)
            a_parts, b_parts = a_parts + (a_ctx,), b_parts + (b_ctx,)
        n_rows = N_LAT if last else N_TOK
        xf = _merge(xs, a_parts, b_parts, u, cgb, gates, mods, conv_w[l],
                    w_a_b, w_b_b, w_c_b, w_o_b, n_rows, l)
        xf = _ffn(xf, mods, norm2_g[l].reshape(1, D_MODEL), final_g, w_up_b, w_dn_b, last, l)
        xs = (xf,)
    return xf.reshape(BATCH, SEQ, D_MODEL)
```

```python
import functools

import numpy as np
import jax
import jax.numpy as jnp
from jax import lax
from jax.experimental import pallas as pl
from jax.experimental.pallas import tpu as pltpu

F32 = jnp.float32
BF16 = jnp.bfloat16

D_MODEL = 1024
BATCH = 4
SEQ = 8192
DEPTH = 4
GRID_W = 64
GRID_ROWS = SEQ // GRID_W
CTX_LEN = 256
HEAD_DIM = 64
NORM_EPS = 1e-6
NA_HEADS = 8
NA_WIN_ROWS = 8
NA_WIN_COLS = 16
SW_HEADS = 8
SW_WINDOW = 128
ROPE_THETA = 10000.0
ROPE_FREQS = HEAD_DIM // 4
CONV_WIDTH = 512
D_FF = 2816
N_IN = 6912
N_ROT = 768

N_LAT = BATCH * SEQ
N_CTX = BATCH * CTX_LEN
N_TOK = N_LAT + N_CTX

LANES = 128
NEG = -1e30

C_QA, C_KA, C_VA, C_QB, C_KB, C_VB, C_CX, C_CGB, C_CGC, C_ZG = (
    0, 512, 1024, 1536, 2048, 2176, 2304, 2816, 3328, 3840)

TM = 512
MERGE_TM = 1024
FFN_TM = 1024
NA_QROWS = 2
NA_KROWS = NA_QROWS + NA_WIN_ROWS
NA_TQ = NA_QROWS * GRID_W
NA_TK = NA_KROWS * GRID_W
NA_STEPS = GRID_ROWS // NA_QROWS
SW_TQ = 128
SW_TK = 3 * SW_TQ
SW_STEPS = SEQ // SW_TQ
SW_GROUP = 4
VMEM_LIMIT = 56 * 1024 * 1024


def _cparams(n_axes):
    return pltpu.CompilerParams(dimension_semantics=("arbitrary",) * n_axes,
                                vmem_limit_bytes=VMEM_LIMIT)


def _const_spec(shape):
    nd = len(shape)
    return pl.BlockSpec(shape, lambda *_: (0,) * nd, pipeline_mode=pl.Buffered(1))


def _layer_spec(shape, layer):
    nd = len(shape)
    return pl.BlockSpec((None,) + tuple(shape), lambda *_: (layer,) + (0,) * nd, pipeline_mode=pl.Buffered(1))


def _dot(a, b):
    return jnp.dot(a, b, preferred_element_type=F32)


def _dot_nt(a, b):
    return lax.dot_general(a, b, (((1,), (1,)), ((), ())), preferred_element_type=F32)


def _sigmoid(x):
    return 0.5 * jnp.tanh(0.5 * x) + 0.5


def _mod_norm(x, g, shift, scale):
    y = x * lax.rsqrt(jnp.mean(x * x, axis=-1, keepdims=True) + NORM_EPS) * g
    return y * (1.0 + scale) + shift


def _ada_kernel(c_ref, w_ref, b_ref, o_ref):
    c = c_ref[...]
    sc = c * jax.nn.sigmoid(c)
    o_ref[...] = jnp.dot(sc, w_ref[...], preferred_element_type=F32,
                         precision=lax.Precision.HIGHEST) + b_ref[...]


def _ada(cvecs, w_ada, b_ada):
    tn = 1536
    return pl.pallas_call(
        _ada_kernel,
        grid=(DEPTH, 6 * D_MODEL // tn),
        in_specs=[
            pl.BlockSpec((8, D_MODEL), lambda l, n: (0, 0)),
            pl.BlockSpec((None, D_MODEL, tn), lambda l, n: (l, 0, n)),
            pl.BlockSpec((None, 1, tn), lambda l, n: (l, 0, n)),
        ],
        out_specs=pl.BlockSpec((None, 8, tn), lambda l, n: (l, 0, n)),
        out_shape=jax.ShapeDtypeStruct((DEPTH, 8, 6 * D_MODEL), F32),
        compiler_params=_cparams(2),
        name="ada",
    )(cvecs, w_ada, b_ada.reshape(DEPTH, 1, 6 * D_MODEL))


def _mod_row(i, tm):
    lat_blocks = N_LAT // tm
    return jnp.where(i < lat_blocks, i // (SEQ // tm), BATCH)


def _mod_spec(part, tm=TM, block_of=lambda i: i):
    return pl.BlockSpec((None, 1, D_MODEL), lambda i: (_mod_row(block_of(i), tm), 0, part))


def _load_tokens(x_ref, xc_ref):
    if xc_ref is None:
        return x_ref[...]
    return jnp.where(pl.program_id(0) < N_LAT // x_ref.shape[0], x_ref[...], xc_ref[...])


def _load_parts(parts):
    return _load_tokens(parts[0], parts[1] if len(parts) == 2 else None)


def _inproj_kernel(*refs, split):
    n_x = 2 if split else 1
    first_parts, next_parts = refs[:n_x], refs[n_x:2 * n_x]
    (sh0_ref, sc0_ref, shn_ref, scn_ref, g_ref, w_ref, wr_ref, bg_ref, cos_ref, sin_ref,
     qa_ref, ka_ref, va_ref, qb_ref, kb_ref, vb_ref, u_ref, cgb_ref, gate_ref, h_scr, z0_scr) = refs[2 * n_x:]
    i = pl.program_id(0)
    g = g_ref[...]
    q_scale = HEAD_DIM ** -0.5
    z0_cols = slice(C_ZG, C_ZG + D_MODEL)

    def stage(x, shift, scale, slot):
        h = _mod_norm(x, g, shift, scale).astype(BF16)
        h_scr[slot] = h
        z0_scr[slot] = _dot(h, w_ref[:, z0_cols])

    @pl.when(i == 0)
    def _first():
        stage(first_parts[0][...], sh0_ref[...], sc0_ref[...], 0)

    def body(parity):
        h = h_scr[parity]

        def proj(c0, c1):
            return _dot(h, w_ref[:, c0:c1])

        def gates(k, z):
            cols = slice(D_MODEL * k, D_MODEL * (k + 1))
            gate_ref[:, cols] = _sigmoid(z + bg_ref[:, cols]).astype(BF16)

        gates(0, z0_scr[parity])
        for k in (1, 2):
            gates(k, proj(C_ZG + D_MODEL * k, C_ZG + D_MODEL * (k + 1)))

        cosv = cos_ref[...]
        sinv = sin_ref[...]

        def rope(t):
            return t * cosv + pltpu.roll(t, LANES // 2, 1) * sinv

        rot = _dot(h, wr_ref[...])
        for p in range(4):
            qb_ref[:, LANES * p:LANES * (p + 1)] = (
                rope(rot[:, LANES * p:LANES * (p + 1)]) * q_scale).astype(BF16)
        kb_ref[...] = rope(rot[:, 4 * LANES:5 * LANES]).astype(BF16)
        vb_ref[...] = rot[:, 5 * LANES:].astype(BF16)

        u_ref[...] = (proj(C_CGC, C_ZG) * proj(C_CX, C_CGB)).astype(BF16)
        qa_ref[...] = (proj(C_QA, C_KA) * q_scale).astype(BF16)
        cgb_ref[...] = proj(C_CGB, C_CGC).astype(BF16)
        ka_ref[...] = proj(C_KA, C_VA).astype(BF16)
        va_ref[...] = proj(C_VA, C_QB).astype(BF16)

        nxt = jnp.minimum(i + 1, pl.num_programs(0) - 1)
        if split:
            x_next = jnp.where(nxt < N_LAT // TM, next_parts[0][...], next_parts[1][...])
        else:
            x_next = next_parts[0][...]
        stage(x_next, shn_ref[...], scn_ref[...], 1 - parity)

    for parity in range(2):
        pl.when(i % 2 == parity)(functools.partial(body, parity))


def _token_specs(split, tm=TM, width=D_MODEL, block_of=lambda i: i):
    lat_blocks = N_LAT // tm
    if not split:
        return [pl.BlockSpec((tm, width), lambda i: (block_of(i), 0))]
    return [pl.BlockSpec((tm, width), lambda i: (jnp.minimum(block_of(i), lat_blocks - 1), 0)),
            pl.BlockSpec((tm, width), lambda i: (jnp.maximum(block_of(i) - lat_blocks, 0), 0),
                         pipeline_mode=pl.Buffered(1))]


def _inproj(xs, mods, g, w, w_rot, bg, rope_c, rope_s, layer):
    lat_blocks = N_LAT // TM
    seq_blocks = SEQ // TM
    split = len(xs) == 2

    def rope_idx(i):
        return (jnp.where(i < lat_blocks, i % seq_blocks, seq_blocks), 0)

    n_blocks = N_TOK // TM

    def first(i):
        return 0 * i

    def nxt(i):
        return jnp.minimum(i + 1, n_blocks - 1)

    widths = (512, 512, 512, 512, 128, 128, 512, 512, 3072)
    return pl.pallas_call(
        functools.partial(_inproj_kernel, split=split),
        grid=(n_blocks,),
        in_specs=_token_specs(split, block_of=first) + _token_specs(split, block_of=nxt) + [
            _mod_spec(0, block_of=first), _mod_spec(1, block_of=first),
            _mod_spec(0, block_of=nxt), _mod_spec(1, block_of=nxt),
            _const_spec((1, D_MODEL)),
            _layer_spec((D_MODEL, N_IN), layer),
            _layer_spec((D_MODEL, N_ROT), layer),
            _const_spec((1, 3 * D_MODEL)),
            pl.BlockSpec((TM, LANES), rope_idx),
            pl.BlockSpec((TM, LANES), rope_idx),
        ],
        out_specs=[pl.BlockSpec((TM, wd), lambda i: (i, 0)) for wd in widths],
        out_shape=[jax.ShapeDtypeStruct((N_TOK, wd), BF16) for wd in widths],
        scratch_shapes=[pltpu.VMEM((2, TM, D_MODEL), BF16), pltpu.VMEM((2, TM, D_MODEL), F32)],
        compiler_params=_cparams(1),
        name="inproj",
    )(*xs, *xs, mods, mods, mods, mods, g, w, w_rot, bg, rope_c, rope_s)


ATTN_SUB = 4
ATTN_SLOTS = 4
ATTN_LAG = 4
ATTN_C_LAG = 2


def _lane_iota():
    return lax.broadcasted_iota(jnp.int32, (1, LANES), 1)


def _pipeline_step(step, stage_a, stage_b, stage_c, stage_d):
    assert ATTN_SUB == ATTN_SLOTS and ATTN_LAG == ATTN_SUB
    for k in range(ATTN_SUB):
        n = ATTN_SUB * step + k
        stage_d(k, n, k)
        stage_c((k - ATTN_C_LAG) % ATTN_SLOTS, n, k)
        stage_b((k - 1) % ATTN_SLOTS, n, k)
        stage_a(k, n, k)


def _init_stage_buffers(s_scr, m_scr, pv_scr):
    s_scr[...] = jnp.zeros_like(s_scr)
    m_scr[...] = jnp.zeros_like(m_scr)
    pv_scr[...] = jnp.ones_like(pv_scr)


def _scores(qs, k_all, bias):
    s = _dot_nt(qs, k_all)
    nb = bias.shape[1]
    return jnp.concatenate([s[:, :nb] + bias, s[:, nb:]], axis=1)


def _row_max(s, sink):
    m = jnp.max(s, axis=-1, keepdims=True)
    if sink is not None:
        m = jnp.maximum(m, sink)
    return jnp.broadcast_to(m, (s.shape[0], LANES))


def _exp_pv(s_ref, m, v_all):
    nk = v_all.shape[0]
    e = jnp.concatenate(
        [jnp.exp(s_ref[:, LANES * c:LANES * (c + 1)] - m) for c in range(nk // LANES)], axis=1).astype(BF16)
    return _dot(e, jnp.concatenate([v_all, jnp.ones_like(v_all)], axis=1))


def _batch_spec(rows, width):
    return pl.BlockSpec((rows, width), lambda b, n: (b, 0), pipeline_mode=pl.Buffered(1))


def _ctx_spec(width):
    return pl.BlockSpec((CTX_LEN, width), lambda b, n: (N_LAT // CTX_LEN + b, 0))


def _attn_call(kernel_fn, name, operands, lead_specs, kv_specs, kv_width, tq, n_blocks, scratch):
    width = NA_HEADS * HEAD_DIM
    n_out = N_LAT
    n_steps = n_blocks // ATTN_SUB
    lag_steps = ATTN_LAG // ATTN_SUB
    tq = tq * ATTN_SUB

    def q_idx(b, n):
        return (b * n_steps + jnp.minimum(n, n_steps - 1), 0)

    def o_idx(b, n):
        return (b * n_steps + jnp.maximum(n - lag_steps, 0), 0)

    return pl.pallas_call(
        functools.partial(kernel_fn, n_blocks=n_blocks),
        grid=(BATCH, n_steps + lag_steps),
        in_specs=lead_specs + [
            pl.BlockSpec((tq, width), q_idx),
            *kv_specs,
            _ctx_spec(kv_width), _ctx_spec(kv_width),
        ],
        out_specs=pl.BlockSpec((tq, width), o_idx),
        out_shape=jax.ShapeDtypeStruct((n_out, width), BF16),
        scratch_shapes=scratch,
        compiler_params=_cparams(2),
        name=name,
    )(*operands)


NA_PAIRS = NA_HEADS // 2
NA_NK = NA_TK + CTX_LEN
NA_UROWS = ATTN_SUB * NA_QROWS + NA_WIN_ROWS


def _na_union_row(step, block_lag):
    first_qrow = NA_QROWS * (ATTN_SUB * step - block_lag)
    return jnp.clip(first_qrow - NA_WIN_ROWS // 2, 0, GRID_ROWS - NA_UROWS)


def _na_kernel(bt_ref, q_ref, k_ref, v_ref, kc_ref, vc_ref, o_ref,
               bias_scr, s_scr, m_scr, pv_scr, *, n_blocks):
    step = pl.program_id(1)
    low_half = _lane_iota() < HEAD_DIM

    def window_row(j):
        return jnp.clip(NA_QROWS * j - NA_WIN_ROWS // 2, 0, GRID_ROWS - NA_KROWS)

    def window_start(chain, block_lag):
        kr0 = window_row(jnp.clip(chain, 0, n_blocks - 1))
        union0 = _na_union_row(step, block_lag)
        return pl.multiple_of(jnp.clip(kr0 - union0, 0, NA_UROWS - NA_KROWS) * GRID_W, GRID_W)

    @pl.when(step == 0)
    def _init():
        _init_stage_buffers(s_scr, m_scr, pv_scr)

    first = jnp.minimum(ATTN_SUB * step, n_blocks - ATTN_SUB)

    @pl.when((first <= ATTN_SUB) | (first >= NA_STEPS - ATTN_SUB))
    def _regen():
        for k in range(ATTN_SUB):
            jb = first + k
            kr0 = window_row(jb)
            for i in range(NA_QROWS):
                qr = NA_QROWS * jb + i
                r0 = jnp.clip(qr - NA_WIN_ROWS // 2, 0, GRID_ROWS - NA_WIN_ROWS)
                for jj in range(0, NA_KROWS, 2):
                    tiles = []
                    for kk in (jj, jj + 1):
                        kr = kr0 + kk
                        valid = (kr >= r0) & (kr < r0 + NA_WIN_ROWS)
                        idx = jnp.clip(kr - qr + NA_WIN_ROWS - 1, 0, 2 * NA_WIN_ROWS - 2)
                        tiles.append(jnp.where(valid, bt_ref[idx], NEG))
                    bias_scr[k, :, GRID_W * i:GRID_W * (i + 1), GRID_W * jj:GRID_W * (jj + 2)] = (
                        jnp.concatenate(tiles, axis=-1))

    def lanes(p):
        return slice(LANES * p, LANES * (p + 1))

    def sub_rows(k):
        return slice(NA_TQ * k, NA_TQ * (k + 1))

    def stage_a(slot, n, k):
        ks = window_start(n, 0)
        for p in range(NA_PAIRS):
            qp = q_ref[sub_rows(k), lanes(p)]
            qs = jnp.concatenate([jnp.where(low_half, qp, jnp.zeros_like(qp)),
                                  jnp.where(low_half, jnp.zeros_like(qp), qp)], axis=0)
            k_all = jnp.concatenate([k_ref[pl.ds(ks, NA_TK), lanes(p)], kc_ref[:, lanes(p)]], axis=0)
            s_scr[slot, p] = _scores(qs, k_all, bias_scr[k, 2 * p:2 * p + 2].reshape(2 * NA_TQ, NA_TK))

    def stage_b(slot, n, k):
        for p in range(NA_PAIRS):
            m_scr[slot, p] = _row_max(s_scr[slot, p], None)

    def stage_c(slot, n, k):
        ks = window_start(n - ATTN_C_LAG, ATTN_C_LAG)
        for p in range(NA_PAIRS):
            v_all = jnp.concatenate([v_ref[pl.ds(ks, NA_TK), lanes(p)], vc_ref[:, lanes(p)]], axis=0)
            pv_scr[slot, p] = _exp_pv(s_scr.at[slot, p], m_scr[slot, p], v_all)

    def stage_d(slot, n, k):
        for p in range(NA_PAIRS):
            pv = pv_scr[slot, p]
            out = pv[:, :LANES] / pv[:, LANES:]
            o_ref[sub_rows(k), lanes(p)] = jnp.where(low_half, out[:NA_TQ], out[NA_TQ:]).astype(o_ref.dtype)

    _pipeline_step(step, stage_a, stage_b, stage_c, stage_d)


def _na_attention(qa, ka, va, btab, layer):
    n_blocks = NA_STEPS
    rows = 2 * NA_TQ
    scratch = [
        pltpu.VMEM((ATTN_SUB, NA_HEADS, NA_TQ, NA_TK), F32),
        pltpu.VMEM((ATTN_SLOTS, NA_PAIRS, rows, NA_NK), F32),
        pltpu.VMEM((ATTN_SLOTS, NA_PAIRS, rows, LANES), F32),
        pltpu.VMEM((ATTN_SLOTS, NA_PAIRS, rows, 2 * LANES), F32),
    ]
    bt_spec = [_layer_spec((2 * NA_WIN_ROWS - 1, NA_HEADS, GRID_W, GRID_W), layer)]
    width = NA_HEADS * HEAD_DIM

    def window_spec(block_lag):
        return pl.BlockSpec(
            (pl.Element(NA_UROWS * GRID_W), pl.Element(width)),
            lambda b, s: (pl.multiple_of(b * SEQ + _na_union_row(s, block_lag) * GRID_W, GRID_W), 0))

    return _attn_call(_na_kernel, "na_attn", (btab, qa, ka, va, ka, va), bt_spec,
                      [window_spec(0), window_spec(ATTN_C_LAG)], width, NA_TQ, n_blocks, scratch)


SW_NK = SW_TK + CTX_LEN
SW_KV = SW_HEADS // SW_GROUP


def _swa_kernel(sink_ref, q_ref, k_ref, v_ref, kc_ref, vc_ref, o_ref,
                mask_scr, s_scr, m_scr, pv_scr, *, n_blocks):
    step = pl.program_id(1)
    lane = _lane_iota()
    low_half = lane < HEAD_DIM
    group0 = (lane % HEAD_DIM) < HEAD_DIM // 2

    def window_row(j):
        return jnp.clip(j * SW_TQ - SW_TQ, 0, SEQ - SW_TK)

    def window_start(chain):
        return pl.multiple_of(window_row(jnp.clip(chain, 0, n_blocks - 1)), SW_TQ)

    def sink_rows(g, width):
        return jnp.concatenate(
            [jnp.full((SW_TQ, width), sink_ref[p + SW_GROUP * g], F32) for p in range(SW_GROUP)], axis=0)

    def lanes(p):
        return slice(LANES * p, LANES * (p + 1))

    @pl.when(step == 0)
    def _init():
        _init_stage_buffers(s_scr, m_scr, pv_scr)

    first = jnp.minimum(ATTN_SUB * step, n_blocks - ATTN_SUB)

    @pl.when((first <= ATTN_SUB) | (first >= SW_STEPS - ATTN_SUB))
    def _regen():
        for k in range(ATTN_SUB):
            jb = first + k
            rel = ((window_row(jb) - jb * SW_TQ) + lax.broadcasted_iota(jnp.int32, (SW_TQ, SW_TK), 1)
                   - lax.broadcasted_iota(jnp.int32, (SW_TQ, SW_TK), 0))
            band = jnp.where(jnp.abs(rel) <= SW_WINDOW, 0.0, NEG)
            for p in range(SW_GROUP):
                mask_scr[k, SW_TQ * p:SW_TQ * (p + 1), :] = band

    def sub_rows(k):
        return slice(SW_TQ * k, SW_TQ * (k + 1))

    def stage_a(slot, n, k):
        ks = window_start(n)
        k_all = jnp.concatenate([k_ref[pl.ds(ks, SW_TK), :], kc_ref[...]], axis=0)
        for g in range(SW_KV):
            in_group = group0 if g == 0 else ~group0
            qs = jnp.concatenate(
                [jnp.where(in_group, q_ref[sub_rows(k), lanes(p)], jnp.zeros((SW_TQ, LANES), BF16))
                 for p in range(SW_GROUP)], axis=0)
            s_scr[slot, g] = _scores(qs, k_all, mask_scr[k])

    def stage_b(slot, n, k):
        for g in range(SW_KV):
            m_scr[slot, g] = _row_max(s_scr[slot, g], sink_rows(g, 1))

    def stage_c(slot, n, k):
        ks = window_start(n - ATTN_C_LAG)
        v_all = jnp.concatenate([v_ref[pl.ds(ks, SW_TK), :], vc_ref[...]], axis=0)
        for g in range(SW_KV):
            m = m_scr[slot, g]
            pv = _exp_pv(s_scr.at[slot, g], m, v_all)
            pv_scr[slot, g, :, :LANES] = pv[:, :LANES]
            pv_scr[slot, g, :, LANES:] = pv[:, LANES:] + jnp.exp(sink_rows(g, LANES) - m)

    def stage_d(slot, n, k):
        outs = [pv_scr[slot, g, :, :LANES] / pv_scr[slot, g, :, LANES:] for g in range(SW_KV)]
        for p in range(SW_GROUP):
            blk = slice(SW_TQ * p, SW_TQ * (p + 1))
            o_ref[sub_rows(k), lanes(p)] = jnp.where(low_half, outs[0][blk], outs[1][blk]).astype(o_ref.dtype)

    _pipeline_step(step, stage_a, stage_b, stage_c, stage_d)


def _swa_attention(sink, qb, kb, vb):
    n_blocks = SW_STEPS
    rows = SW_GROUP * SW_TQ
    scratch = [
        pltpu.VMEM((ATTN_SUB, rows, SW_TK), F32),
        pltpu.VMEM((ATTN_SLOTS, SW_KV, rows, SW_NK), F32),
        pltpu.VMEM((ATTN_SLOTS, SW_KV, rows, LANES), F32),
        pltpu.VMEM((ATTN_SLOTS, SW_KV, rows, 2 * LANES), F32),
    ]
    return _attn_call(_swa_kernel, "swa_attn", (sink, qb, kb, vb, kb, vb),
                      [pl.BlockSpec(memory_space=pltpu.SMEM)], [_batch_spec(SEQ, LANES)] * 2,
                      LANES, SW_TQ, n_blocks, scratch)


def _ctx_attn_kernel(sink_ref, qa_ref, ka_ref, va_ref, qb_ref, kb_ref, vb_ref, oa_ref, ob_ref):
    lane = _lane_iota()
    low_half = lane < HEAD_DIM
    group0 = (lane % HEAD_DIM) < HEAD_DIM // 2

    def lanes(p):
        return slice(LANES * p, LANES * (p + 1))

    def attend(qs, k, v, sink):
        s = _dot_nt(qs, k)
        m = _row_max(s, sink)
        e = jnp.concatenate([jnp.exp(s[:, LANES * c:LANES * (c + 1)] - m) for c in range(CTX_LEN // LANES)],
                            axis=1).astype(BF16)
        pv = _dot(e, jnp.concatenate([v, jnp.ones_like(v)], axis=1))
        den = pv[:, LANES:]
        if sink is not None:
            den = den + jnp.exp(sink - m)
        return pv[:, :LANES] / den

    for p in range(NA_PAIRS):
        qp = qa_ref[:, lanes(p)]
        qs = jnp.concatenate([jnp.where(low_half, qp, jnp.zeros_like(qp)),
                              jnp.where(low_half, jnp.zeros_like(qp), qp)], axis=0)
        out = attend(qs, ka_ref[:, lanes(p)], va_ref[:, lanes(p)], None)
        oa_ref[:, lanes(p)] = jnp.where(low_half, out[:CTX_LEN], out[CTX_LEN:]).astype(oa_ref.dtype)

    outs = []
    for g in range(SW_KV):
        in_group = group0 if g == 0 else ~group0
        qs = jnp.concatenate(
            [jnp.where(in_group, qb_ref[:, lanes(p)], jnp.zeros((CTX_LEN, LANES), BF16))
             for p in range(SW_GROUP)], axis=0)
        sink = jnp.concatenate(
            [jnp.full((CTX_LEN, 1), sink_ref[p + SW_GROUP * g], F32) for p in range(SW_GROUP)], axis=0)
        outs.append(attend(qs, kb_ref[...], vb_ref[...], sink))
    for p in range(SW_GROUP):
        blk = slice(CTX_LEN * p, CTX_LEN * (p + 1))
        ob_ref[:, lanes(p)] = jnp.where(low_half, outs[0][blk], outs[1][blk]).astype(ob_ref.dtype)


def _ctx_attention(sink, qa, ka, va, qb, kb, vb):
    width = NA_HEADS * HEAD_DIM
    ctx0 = N_LAT // CTX_LEN

    def ctx_rows(w):
        return pl.BlockSpec((CTX_LEN, w), lambda b: (ctx0 + b, 0))

    return pl.pallas_call(
        _ctx_attn_kernel,
        grid=(BATCH,),
        in_specs=[pl.BlockSpec(memory_space=pltpu.SMEM),
                  ctx_rows(width), ctx_rows(width), ctx_rows(width),
                  ctx_rows(width), ctx_rows(LANES), ctx_rows(LANES)],
        out_specs=[pl.BlockSpec((CTX_LEN, width), lambda b: (b, 0))] * 2,
        out_shape=[jax.ShapeDtypeStruct((N_CTX, width), BF16)] * 2,
        compiler_params=_cparams(1),
        name="ctx_attn",
    )(sink, qa, ka, va, qb, kb, vb)


HALO = 16


def _merge_kernel(*refs, n_x, n_ab):
    x_parts, a_parts, b_parts = refs[:n_x], refs[n_x:n_x + n_ab], refs[n_x + n_ab:n_x + 2 * n_ab]
    (u_ref, up_ref, un_ref, cgb_ref, gate_ref, g1_ref, cw_ref,
     wa_ref, wb_ref, wc_ref, wo_ref, o_ref) = refs[n_x + 2 * n_ab:]
    i = pl.program_id(0)
    local = lax.broadcasted_iota(jnp.int32, (MERGE_TM, 1), 0)
    row = i * MERGE_TM + local
    is_lat = row < N_LAT
    pos = jnp.where(is_lat, row & (SEQ - 1), (row - N_LAT) & (CTX_LEN - 1))
    seq_first = pos == 0
    seq_last = pos == jnp.where(is_lat, SEQ - 1, CTX_LEN - 1)

    u = u_ref[...].astype(F32)
    prev_row = up_ref[...].astype(F32)[HALO - 1:HALO, :]
    next_row = un_ref[...].astype(F32)[0:1, :]
    u_prev = jnp.where(local == 0, prev_row, pltpu.roll(u, 1, 0))
    u_prev = jnp.where(seq_first, 0.0, u_prev)
    u_next = jnp.where(local == MERGE_TM - 1, next_row, pltpu.roll(u, MERGE_TM - 1, 0))
    u_next = jnp.where(seq_last, 0.0, u_next)
    cw = cw_ref[...]
    y = cw[0:1, :] * u_prev + cw[1:2, :] * u + cw[2:3, :] * u_next
    c_out = (cgb_ref[...].astype(F32) * y).astype(BF16)

    merged = gate_ref[:, 0:D_MODEL].astype(F32) * _dot(_load_parts(a_parts), wa_ref[...])
    merged = merged + gate_ref[:, D_MODEL:2 * D_MODEL].astype(F32) * _dot(_load_parts(b_parts), wb_ref[...])
    merged = merged + gate_ref[:, 2 * D_MODEL:3 * D_MODEL].astype(F32) * _dot(c_out, wc_ref[...])
    o_ref[...] = _load_parts(x_parts) + g1_ref[...] * _dot(merged.astype(BF16), wo_ref[...])


def _merge(xs, a_parts, b_parts, u, cgb, gates, mods, conv_w, wa, wb, wc, wo, n_rows, layer):
    halo_blocks = N_TOK // HALO
    per = MERGE_TM // HALO
    width = NA_HEADS * HEAD_DIM
    ab_specs = _token_specs(len(a_parts) == 2, MERGE_TM, width)

    def tok(wd):
        return pl.BlockSpec((MERGE_TM, wd), lambda i: (i, 0))

    return pl.pallas_call(
        functools.partial(_merge_kernel, n_x=len(xs), n_ab=len(a_parts)),
        grid=(n_rows // MERGE_TM,),
        in_specs=_token_specs(len(xs) == 2, MERGE_TM) + ab_specs + ab_specs + [
            tok(CONV_WIDTH),
            pl.BlockSpec((HALO, CONV_WIDTH), lambda i: (jnp.maximum(i * per - 1, 0), 0)),
            pl.BlockSpec((HALO, CONV_WIDTH), lambda i: (jnp.minimum((i + 1) * per, halo_blocks - 1), 0)),
            tok(CONV_WIDTH), tok(3 * D_MODEL),
            _mod_spec(2, MERGE_TM),
            _const_spec((3, CONV_WIDTH)),
            _layer_spec((512, D_MODEL), layer), _layer_spec((512, D_MODEL), layer),
            _layer_spec((512, D_MODEL), layer), _layer_spec((D_MODEL, D_MODEL), layer),
        ],
        out_specs=tok(D_MODEL),
        out_shape=jax.ShapeDtypeStruct((n_rows, D_MODEL), F32),
        compiler_params=_cparams(1),
        name="merge",
    )(*xs, *a_parts, *b_parts, u, u, u, cgb, gates, mods, conv_w, wa, wb, wc, wo)


FF_CHUNKS = tuple((c, min(c + 512, D_FF)) for c in range(0, D_FF, 512))


def _ffn_kernel(x_ref, sh_ref, sc_ref, g2_ref, ng_ref, fg_ref, wu_ref, wd_ref, o_ref, *, final):
    x = x_ref[...]
    h = _mod_norm(x, ng_ref[...], sh_ref[...], sc_ref[...]).astype(BF16)
    acc = None
    for c0, c1 in FF_CHUNKS:
        gate = _dot(h, wu_ref[:, c0:c1])
        up = _dot(h, wu_ref[:, D_FF + c0:D_FF + c1])
        act = (gate * _sigmoid(gate) * up).astype(BF16)
        part = _dot(act, wd_ref[c0:c1, :])
        acc = part if acc is None else acc + part
    xn = x + g2_ref[...] * acc
    if final:
        xn = xn * lax.rsqrt(jnp.mean(xn * xn, axis=-1, keepdims=True) + NORM_EPS) * fg_ref[...]
    o_ref[...] = xn


def _ffn(xf, mods, ng, fg, wu, wd, final, layer):
    n_rows = xf.shape[0]
    return pl.pallas_call(
        functools.partial(_ffn_kernel, final=final),
        grid=(n_rows // FFN_TM,),
        in_specs=[
            pl.BlockSpec((FFN_TM, D_MODEL), lambda i: (i, 0)),
            _mod_spec(3, FFN_TM), _mod_spec(4, FFN_TM), _mod_spec(5, FFN_TM),
            _const_spec((1, D_MODEL)), _const_spec((1, D_MODEL)),
            _layer_spec((D_MODEL, 2 * D_FF), layer), _layer_spec((D_FF, D_MODEL), layer),
        ],
        out_specs=pl.BlockSpec((FFN_TM, D_MODEL), lambda i: (i, 0)),
        out_shape=jax.ShapeDtypeStruct((n_rows, D_MODEL), F32),
        compiler_params=_cparams(1),
        name="ffn",
    )(xf, mods, mods, mods, ng, fg, wu, wd)


def _prep_w_rot(w_in):
    half = HEAD_DIM // 2
    lead = w_in.shape[:-1]
    qb = w_in[..., C_QB:C_KB].reshape(*lead, 2, SW_GROUP, 2, half)
    qb = jnp.moveaxis(qb, -4, -2).reshape(*lead, SW_HEADS * HEAD_DIM)
    kb = w_in[..., C_KB:C_VB].reshape(*lead, 2, 2, half)
    kb = jnp.swapaxes(kb, -3, -2).reshape(*lead, 2 * HEAD_DIM)
    return jnp.concatenate([qb, kb, w_in[..., C_VB:C_CX]], axis=-1).astype(BF16)


def _prep_w_br_b(w_br_b):
    w = w_br_b.reshape(DEPTH, 2, SW_GROUP, HEAD_DIM, D_MODEL)
    return jnp.swapaxes(w, 1, 2).reshape(DEPTH, SW_HEADS * HEAD_DIM, D_MODEL).astype(BF16)


def _rope_tables():
    t = np.arange(SEQ)
    row = (t // GRID_W).astype(np.float32)
    col = (t % GRID_W).astype(np.float32)
    inv = jnp.asarray(ROPE_THETA, F32) ** (-jnp.arange(ROPE_FREQS, dtype=F32) / ROPE_FREQS)
    ang = jnp.concatenate([jnp.asarray(row)[:, None] * inv, jnp.asarray(col)[:, None] * inv], axis=-1)
    cos = jnp.cos(ang)
    sin = jnp.sin(ang)
    cos4 = jnp.concatenate([cos, cos, cos, cos], axis=-1)
    sin4 = jnp.concatenate([-sin, -sin, sin, sin], axis=-1)
    cos4 = jnp.concatenate([cos4, jnp.ones((TM, LANES), F32)], axis=0)
    sin4 = jnp.concatenate([sin4, jnp.zeros((TM, LANES), F32)], axis=0)
    return cos4, sin4


def _bias_tiles(rpb):
    qc = np.arange(GRID_W)[:, None]
    kc = np.arange(GRID_W)[None, :]
    start = np.clip(qc - NA_WIN_COLS // 2, 0, GRID_W - NA_WIN_COLS)
    valid = (kc >= start) & (kc < start + NA_WIN_COLS)
    dcol = kc - qc + NA_WIN_COLS - 1
    onehot = ((dcol[None] == np.arange(2 * NA_WIN_COLS - 1)[:, None, None]) & valid[None]).astype(np.float32)
    tiles = jnp.einsum("lhrd,dqk->lrhqk", rpb.astype(F32), jnp.asarray(onehot),
                       precision=lax.Precision.HIGHEST)
    return tiles + jnp.asarray(np.where(valid, 0.0, NEG).astype(np.float32))


def kernel(x, c, ctx, c_ctx, w_ada, b_ada, norm1_g, norm2_g, w_in, b_gate, na_rpb, sw_sink, conv_w,
           w_br_a, w_br_b, w_br_c, w_o, w_ffn_up, w_ffn_down, final_norm_g):
    xs = (x.reshape(N_LAT, D_MODEL), ctx.reshape(N_CTX, D_MODEL))
    cvecs = jnp.concatenate([c, c_ctx[None, :], jnp.zeros((8 - BATCH - 1, D_MODEL), F32)], axis=0)
    mods_all = _ada(cvecs, w_ada, b_ada).reshape(DEPTH, 8, 1, 6 * D_MODEL)

    rope_c, rope_s = _rope_tables()
    final_g = final_norm_g.reshape(1, D_MODEL)
    w_in_b, w_rot_b = w_in.astype(BF16), _prep_w_rot(w_in)
    w_a_b, w_b_b, w_c_b = w_br_a.astype(BF16), _prep_w_br_b(w_br_b), w_br_c.astype(BF16)
    w_o_b, w_up_b, w_dn_b = w_o.astype(BF16), w_ffn_up.astype(BF16), w_ffn_down.astype(BF16)
    btab = _bias_tiles(na_rpb)

    for l in range(DEPTH):
        last = l == DEPTH - 1
        mods = mods_all[l]
        qa, ka, va, qb, kb, vb, u, cgb, gates = _inproj(
            xs, mods, norm1_g[l].reshape(1, D_MODEL), w_in_b, w_rot_b,
            b_gate[l].reshape(1, 3 * D_MODEL), rope_c, rope_s, l)
        a_parts = (_na_attention(qa, ka, va, btab, l),)
        b_parts = (_swa_attention(sw_sink[l], qb, kb, vb),)
        if not last:
            a_ctx, b_ctx = _ctx_attention(sw_sink[l], qa, ka, va, qb, kb, vb)
            a_parts, b_parts = a_parts + (a_ctx,), b_parts + (b_ctx,)
        n_rows = N_LAT if last else N_TOK
        xf = _merge(xs, a_parts, b_parts, u, cgb, gates, mods, conv_w[l],
                    w_a_b, w_b_b, w_c_b, w_o_b, n_rows, l)
        xf = _ffn(xf, mods, norm2_g[l].reshape(1, D_MODEL), final_g, w_up_b, w_dn_b, last, l)
        xs = (xf,)
    return xf.reshape(BATCH, SEQ, D_MODEL)
```

```python
import functools

import numpy as np
import jax
import jax.numpy as jnp
from jax import lax
from jax.experimental import pallas as pl
from jax.experimental.pallas import tpu as pltpu

F32 = jnp.float32
BF16 = jnp.bfloat16

D_MODEL = 1024
BATCH = 4
SEQ = 8192
DEPTH = 4
GRID_W = 64
GRID_ROWS = SEQ // GRID_W
CTX_LEN = 256
HEAD_DIM = 64
NORM_EPS = 1e-6
NA_HEADS = 8
NA_WIN_ROWS = 8
NA_WIN_COLS = 16
SW_HEADS = 8
SW_WINDOW = 128
ROPE_THETA = 10000.0
ROPE_FREQS = HEAD_DIM // 4
CONV_WIDTH = 512
D_FF = 2816
N_IN = 6912
N_ROT = 768

N_LAT = BATCH * SEQ
N_CTX = BATCH * CTX_LEN
N_TOK = N_LAT + N_CTX

LANES = 128
NEG = -1e30

C_QA, C_KA, C_VA, C_QB, C_KB, C_VB, C_CX, C_CGB, C_CGC, C_ZG = (
    0, 512, 1024, 1536, 2048, 2176, 2304, 2816, 3328, 3840)

TM = 512
MERGE_TM = 1024
FFN_TM = 1024
NA_QROWS = 2
NA_KROWS = NA_QROWS + NA_WIN_ROWS
NA_TQ = NA_QROWS * GRID_W
NA_TK = NA_KROWS * GRID_W
NA_STEPS = GRID_ROWS // NA_QROWS
SW_TQ = 128
SW_TK = 3 * SW_TQ
SW_STEPS = SEQ // SW_TQ
SW_GROUP = 4
VMEM_LIMIT = 56 * 1024 * 1024


def _cparams(n_axes):
    return pltpu.CompilerParams(dimension_semantics=("arbitrary",) * n_axes,
                                vmem_limit_bytes=VMEM_LIMIT)


def _const_spec(shape):
    nd = len(shape)
    return pl.BlockSpec(shape, lambda *_: (0,) * nd, pipeline_mode=pl.Buffered(1))


def _layer_spec(shape, layer):
    nd = len(shape)
    return pl.BlockSpec((None,) + tuple(shape), lambda *_: (layer,) + (0,) * nd, pipeline_mode=pl.Buffered(1))


def _dot(a, b):
    return jnp.dot(a, b, preferred_element_type=F32)


def _dot_nt(a, b):
    return lax.dot_general(a, b, (((1,), (1,)), ((), ())), preferred_element_type=F32)


def _sigmoid(x):
    return 0.5 * jnp.tanh(0.5 * x) + 0.5


def _mod_norm(x, g, shift, scale):
    y = x * lax.rsqrt(jnp.mean(x * x, axis=-1, keepdims=True) + NORM_EPS) * g
    return y * (1.0 + scale) + shift


def _ada_kernel(c_ref, w_ref, b_ref, o_ref):
    c = c_ref[...]
    sc = c * jax.nn.sigmoid(c)
    o_ref[...] = jnp.dot(sc, w_ref[...], preferred_element_type=F32,
                         precision=lax.Precision.HIGHEST) + b_ref[...]


def _ada(cvecs, w_ada, b_ada):
    tn = 1536
    return pl.pallas_call(
        _ada_kernel,
        grid=(DEPTH, 6 * D_MODEL // tn),
        in_specs=[
            pl.BlockSpec((8, D_MODEL), lambda l, n: (0, 0)),
            pl.BlockSpec((None, D_MODEL, tn), lambda l, n: (l, 0, n)),
            pl.BlockSpec((None, 1, tn), lambda l, n: (l, 0, n)),
        ],
        out_specs=pl.BlockSpec((None, 8, tn), lambda l, n: (l, 0, n)),
        out_shape=jax.ShapeDtypeStruct((DEPTH, 8, 6 * D_MODEL), F32),
        compiler_params=_cparams(2),
        name="ada",
    )(cvecs, w_ada, b_ada.reshape(DEPTH, 1, 6 * D_MODEL))


def _mod_row(i, tm):
    lat_blocks = N_LAT // tm
    return jnp.where(i < lat_blocks, i // (SEQ // tm), BATCH)


def _mod_spec(part, tm=TM, block_of=lambda i: i):
    return pl.BlockSpec((None, 1, D_MODEL), lambda i: (_mod_row(block_of(i), tm), 0, part))


def _load_tokens(x_ref, xc_ref):
    if xc_ref is None:
        return x_ref[...]
    return jnp.where(pl.program_id(0) < N_LAT // x_ref.shape[0], x_ref[...], xc_ref[...])


def _load_parts(parts):
    return _load_tokens(parts[0], parts[1] if len(parts) == 2 else None)


def _inproj_kernel(*refs, split):
    n_x = 2 if split else 1
    first_parts, next_parts = refs[:n_x], refs[n_x:2 * n_x]
    (sh0_ref, sc0_ref, shn_ref, scn_ref, g_ref, w_ref, wr_ref, bg_ref, cos_ref, sin_ref,
     qa_ref, ka_ref, va_ref, qb_ref, kb_ref, vb_ref, u_ref, cgb_ref, gate_ref, h_scr, z0_scr) = refs[2 * n_x:]
    i = pl.program_id(0)
    g = g_ref[...]
    q_scale = HEAD_DIM ** -0.5
    z0_cols = slice(C_ZG, C_ZG + D_MODEL)

    def stage(x, shift, scale, slot):
        h = _mod_norm(x, g, shift, scale).astype(BF16)
        h_scr[slot] = h
        z0_scr[slot] = _dot(h, w_ref[:, z0_cols])

    @pl.when(i == 0)
    def _first():
        stage(first_parts[0][...], sh0_ref[...], sc0_ref[...], 0)

    def body(parity):
        h = h_scr[parity]

        def proj(c0, c1):
            return _dot(h, w_ref[:, c0:c1])

        def gates(k, z):
            cols = slice(D_MODEL * k, D_MODEL * (k + 1))
            gate_ref[:, cols] = _sigmoid(z + bg_ref[:, cols]).astype(BF16)

        gates(0, z0_scr[parity])
        for k in (1, 2):
            gates(k, proj(C_ZG + D_MODEL * k, C_ZG + D_MODEL * (k + 1)))

        cosv = cos_ref[...]
        sinv = sin_ref[...]

        def rope(t):
            return t * cosv + pltpu.roll(t, LANES // 2, 1) * sinv

        rot = _dot(h, wr_ref[...])
        for p in range(4):
            qb_ref[:, LANES * p:LANES * (p + 1)] = (
                rope(rot[:, LANES * p:LANES * (p + 1)]) * q_scale).astype(BF16)
        kb_ref[...] = rope(rot[:, 4 * LANES:5 * LANES]).astype(BF16)
        vb_ref[...] = rot[:, 5 * LANES:].astype(BF16)

        u_ref[...] = (proj(C_CGC, C_ZG) * proj(C_CX, C_CGB)).astype(BF16)
        qa_ref[...] = (proj(C_QA, C_KA) * q_scale).astype(BF16)
        cgb_ref[...] = proj(C_CGB, C_CGC).astype(BF16)
        ka_ref[...] = proj(C_KA, C_VA).astype(BF16)
        va_ref[...] = proj(C_VA, C_QB).astype(BF16)

        nxt = jnp.minimum(i + 1, pl.num_programs(0) - 1)
        if split:
            x_next = jnp.where(nxt < N_LAT // TM, next_parts[0][...], next_parts[1][...])
        else:
            x_next = next_parts[0][...]
        stage(x_next, shn_ref[...], scn_ref[...], 1 - parity)

    for parity in range(2):
        pl.when(i % 2 == parity)(functools.partial(body, parity))


def _token_specs(split, tm=TM, width=D_MODEL, block_of=lambda i: i):
    lat_blocks = N_LAT // tm
    if not split:
        return [pl.BlockSpec((tm, width), lambda i: (block_of(i), 0))]
    return [pl.BlockSpec((tm, width), lambda i: (jnp.minimum(block_of(i), lat_blocks - 1), 0)),
            pl.BlockSpec((tm, width), lambda i: (jnp.maximum(block_of(i) - lat_blocks, 0), 0),
                         pipeline_mode=pl.Buffered(1))]


def _inproj(xs, mods, g, w, w_rot, bg, rope_c, rope_s, layer):
    lat_blocks = N_LAT // TM
    seq_blocks = SEQ // TM
    split = len(xs) == 2

    def rope_idx(i):
        return (jnp.where(i < lat_blocks, i % seq_blocks, seq_blocks), 0)

    n_blocks = N_TOK // TM

    def first(i):
        return 0 * i

    def nxt(i):
        return jnp.minimum(i + 1, n_blocks - 1)

    widths = (512, 512, 512, 512, 128, 128, 512, 512, 3072)
    return pl.pallas_call(
        functools.partial(_inproj_kernel, split=split),
        grid=(n_blocks,),
        in_specs=_token_specs(split, block_of=first) + _token_specs(split, block_of=nxt) + [
            _mod_spec(0, block_of=first), _mod_spec(1, block_of=first),
            _mod_spec(0, block_of=nxt), _mod_spec(1, block_of=nxt),
            _const_spec((1, D_MODEL)),
            _layer_spec((D_MODEL, N_IN), layer),
            _layer_spec((D_MODEL, N_ROT), layer),
            _const_spec((1, 3 * D_MODEL)),
            pl.BlockSpec((TM, LANES), rope_idx),
            pl.BlockSpec((TM, LANES), rope_idx),
        ],
        out_specs=[pl.BlockSpec((TM, wd), lambda i: (i, 0)) for wd in widths],
        out_shape=[jax.ShapeDtypeStruct((N_TOK, wd), BF16) for wd in widths],
        scratch_shapes=[pltpu.VMEM((2, TM, D_MODEL), BF16), pltpu.VMEM((2, TM, D_MODEL), F32)],
        compiler_params=_cparams(1),
        name="inproj",
    )(*xs, *xs, mods, mods, mods, mods, g, w, w_rot, bg, rope_c, rope_s)


ATTN_SUB = 4
ATTN_SLOTS = 4
ATTN_LAG = 4
ATTN_C_LAG = 2


def _lane_iota():
    return lax.broadcasted_iota(jnp.int32, (1, LANES), 1)


def _pipeline_step(step, stage_a, stage_b, stage_c, stage_d):
    assert ATTN_SUB == ATTN_SLOTS and ATTN_LAG == ATTN_SUB
    for k in range(ATTN_SUB):
        n = ATTN_SUB * step + k
        stage_d(k, n, k)
        stage_c((k - ATTN_C_LAG) % ATTN_SLOTS, n, k)
        stage_b((k - 1) % ATTN_SLOTS, n, k)
        stage_a(k, n, k)


def _init_stage_buffers(s_scr, m_scr, pv_scr):
    s_scr[...] = jnp.zeros_like(s_scr)
    m_scr[...] = jnp.zeros_like(m_scr)
    pv_scr[...] = jnp.ones_like(pv_scr)


def _scores(qs, k_all, bias):
    s = _dot_nt(qs, k_all)
    nb = bias.shape[1]
    return jnp.concatenate([s[:, :nb] + bias, s[:, nb:]], axis=1)


def _row_max(s, sink):
    m = jnp.max(s, axis=-1, keepdims=True)
    if sink is not None:
        m = jnp.maximum(m, sink)
    return jnp.broadcast_to(m, (s.shape[0], LANES))


def _exp_pv(s_ref, m, v_all):
    nk = v_all.shape[0]
    e = jnp.concatenate(
        [jnp.exp(s_ref[:, LANES * c:LANES * (c + 1)] - m) for c in range(nk // LANES)], axis=1).astype(BF16)
    return _dot(e, jnp.concatenate([v_all, jnp.ones_like(v_all)], axis=1))


def _batch_spec(rows, width):
    return pl.BlockSpec((rows, width), lambda b, n: (b, 0), pipeline_mode=pl.Buffered(1))


def _ctx_spec(width):
    return pl.BlockSpec((CTX_LEN, width), lambda b, n: (N_LAT // CTX_LEN + b, 0))


def _attn_call(kernel_fn, name, operands, lead_specs, kv_specs, kv_width, tq, n_blocks, scratch):
    width = NA_HEADS * HEAD_DIM
    n_out = N_LAT
    n_steps = n_blocks // ATTN_SUB
    lag_steps = ATTN_LAG // ATTN_SUB
    tq = tq * ATTN_SUB

    def q_idx(b, n):
        return (b * n_steps + jnp.minimum(n, n_steps - 1), 0)

    def o_idx(b, n):
        return (b * n_steps + jnp.maximum(n - lag_steps, 0), 0)

    return pl.pallas_call(
        functools.partial(kernel_fn, n_blocks=n_blocks),
        grid=(BATCH, n_steps + lag_steps),
        in_specs=lead_specs + [
            pl.BlockSpec((tq, width), q_idx),
            *kv_specs,
            _ctx_spec(kv_width), _ctx_spec(kv_width),
        ],
        out_specs=pl.BlockSpec((tq, width), o_idx),
        out_shape=jax.ShapeDtypeStruct((n_out, width), BF16),
        scratch_shapes=scratch,
        compiler_params=_cparams(2),
        name=name,
    )(*operands)


NA_PAIRS = NA_HEADS // 2
NA_NK = NA_TK + CTX_LEN
NA_UROWS = ATTN_SUB * NA_QROWS + NA_WIN_ROWS


def _na_union_row(step, block_lag):
    first_qrow = NA_QROWS * (ATTN_SUB * step - block_lag)
    return jnp.clip(first_qrow - NA_WIN_ROWS // 2, 0, GRID_ROWS - NA_UROWS)


def _na_kernel(bt_ref, q_ref, k_ref, v_ref, kc_ref, vc_ref, o_ref,
               bias_scr, s_scr, m_scr, pv_scr, *, n_blocks):
    step = pl.program_id(1)
    low_half = _lane_iota() < HEAD_DIM

    def window_row(j):
        return jnp.clip(NA_QROWS * j - NA_WIN_ROWS // 2, 0, GRID_ROWS - NA_KROWS)

    def window_start(chain, block_lag):
        kr0 = window_row(jnp.clip(chain, 0, n_blocks - 1))
        union0 = _na_union_row(step, block_lag)
        return pl.multiple_of(jnp.clip(kr0 - union0, 0, NA_UROWS - NA_KROWS) * GRID_W, GRID_W)

    @pl.when((pl.program_id(0) == 0) & (step == 0))
    def _init():
        _init_stage_buffers(s_scr, m_scr, pv_scr)

    def at_edge(j):
        return (j < NA_WIN_ROWS // 2 // NA_QROWS) | (j >= NA_STEPS - NA_WIN_ROWS // 2 // NA_QROWS)

    for k in range(ATTN_SUB):
        jb = ATTN_SUB * step + k

        @pl.when((jb < n_blocks) & ((step == 0) | at_edge(jb) | at_edge(jb - ATTN_SUB)))
        def _regen(k=k, jb=jb):
            kr0 = window_row(jb)
            for i in range(NA_QROWS):
                qr = NA_QROWS * jb + i
                r0 = jnp.clip(qr - NA_WIN_ROWS // 2, 0, GRID_ROWS - NA_WIN_ROWS)
                for jj in range(0, NA_KROWS, 2):
                    tiles = []
                    for kk in (jj, jj + 1):
                        kr = kr0 + kk
                        valid = (kr >= r0) & (kr < r0 + NA_WIN_ROWS)
                        idx = jnp.clip(kr - qr + NA_WIN_ROWS - 1, 0, 2 * NA_WIN_ROWS - 2)
                        tiles.append(jnp.where(valid, bt_ref[idx], NEG))
                    bias_scr[k, :, GRID_W * i:GRID_W * (i + 1), GRID_W * jj:GRID_W * (jj + 2)] = (
                        jnp.concatenate(tiles, axis=-1))

    def lanes(p):
        return slice(LANES * p, LANES * (p + 1))

    def sub_rows(k):
        return slice(NA_TQ * k, NA_TQ * (k + 1))

    def stage_a(slot, n, k):
        ks = window_start(n, 0)
        for p in range(NA_PAIRS):
            qp = q_ref[sub_rows(k), lanes(p)]
            qs = jnp.concatenate([jnp.where(low_half, qp, jnp.zeros_like(qp)),
                                  jnp.where(low_half, jnp.zeros_like(qp), qp)], axis=0)
            k_all = jnp.concatenate([k_ref[pl.ds(ks, NA_TK), lanes(p)], kc_ref[:, lanes(p)]], axis=0)
            s_scr[slot, p] = _scores(qs, k_all, bias_scr[k, 2 * p:2 * p + 2].reshape(2 * NA_TQ, NA_TK))

    def stage_b(slot, n, k):
        for p in range(NA_PAIRS):
            m_scr[slot, p] = _row_max(s_scr[slot, p], None)

    def stage_c(slot, n, k):
        ks = window_start(n - ATTN_C_LAG, ATTN_C_LAG)
        for p in range(NA_PAIRS):
            v_all = jnp.concatenate([v_ref[pl.ds(ks, NA_TK), lanes(p)], vc_ref[:, lanes(p)]], axis=0)
            pv_scr[slot, p] = _exp_pv(s_scr.at[slot, p], m_scr[slot, p], v_all)

    def stage_d(slot, n, k):
        for p in range(NA_PAIRS):
            pv = pv_scr[slot, p]
            out = pv[:, :LANES] / pv[:, LANES:]
            o_ref[sub_rows(k), lanes(p)] = jnp.where(low_half, out[:NA_TQ], out[NA_TQ:]).astype(o_ref.dtype)

    _pipeline_step(step, stage_a, stage_b, stage_c, stage_d)


def _na_attention(qa, ka, va, btab, layer):
    n_blocks = NA_STEPS
    rows = 2 * NA_TQ
    scratch = [
        pltpu.VMEM((ATTN_SUB, NA_HEADS, NA_TQ, NA_TK), F32),
        pltpu.VMEM((ATTN_SLOTS, NA_PAIRS, rows, NA_NK), F32),
        pltpu.VMEM((ATTN_SLOTS, NA_PAIRS, rows, LANES), F32),
        pltpu.VMEM((ATTN_SLOTS, NA_PAIRS, rows, 2 * LANES), F32),
    ]
    bt_spec = [_layer_spec((2 * NA_WIN_ROWS - 1, NA_HEADS, GRID_W, GRID_W), layer)]
    width = NA_HEADS * HEAD_DIM

    def window_spec(block_lag):
        return pl.BlockSpec(
            (pl.Element(NA_UROWS * GRID_W), pl.Element(width)),
            lambda b, s: (pl.multiple_of(b * SEQ + _na_union_row(s, block_lag) * GRID_W, GRID_W), 0))

    return _attn_call(_na_kernel, "na_attn", (btab, qa, ka, va, ka, va), bt_spec,
                      [window_spec(0), window_spec(ATTN_C_LAG)], width, NA_TQ, n_blocks, scratch)


SW_NK = SW_TK + CTX_LEN
SW_KV = SW_HEADS // SW_GROUP


def _swa_kernel(sink_ref, q_ref, k_ref, v_ref, kc_ref, vc_ref, o_ref,
                mask_scr, s_scr, m_scr, pv_scr, *, n_blocks):
    step = pl.program_id(1)
    lane = _lane_iota()
    low_half = lane < HEAD_DIM
    group0 = (lane % HEAD_DIM) < HEAD_DIM // 2

    def window_row(j):
        return jnp.clip(j * SW_TQ - SW_TQ, 0, SEQ - SW_TK)

    def window_start(chain):
        return pl.multiple_of(window_row(jnp.clip(chain, 0, n_blocks - 1)), SW_TQ)

    def sink_rows(g, width):
        return jnp.concatenate(
            [jnp.full((SW_TQ, width), sink_ref[p + SW_GROUP * g], F32) for p in range(SW_GROUP)], axis=0)

    def lanes(p):
        return slice(LANES * p, LANES * (p + 1))

    @pl.when((pl.program_id(0) == 0) & (step == 0))
    def _init():
        _init_stage_buffers(s_scr, m_scr, pv_scr)

    def at_edge(j):
        return (j == 0) | (j == SW_STEPS - 1)

    for k in range(ATTN_SUB):
        jb = ATTN_SUB * step + k

        @pl.when((jb < n_blocks) & ((step == 0) | at_edge(jb) | at_edge(jb - ATTN_SUB)))
        def _regen(k=k, jb=jb):
            rel = ((window_row(jb) - jb * SW_TQ) + lax.broadcasted_iota(jnp.int32, (SW_TQ, SW_TK), 1)
                   - lax.broadcasted_iota(jnp.int32, (SW_TQ, SW_TK), 0))
            band = jnp.where(jnp.abs(rel) <= SW_WINDOW, 0.0, NEG)
            for p in range(SW_GROUP):
                mask_scr[k, SW_TQ * p:SW_TQ * (p + 1), :] = band

    def sub_rows(k):
        return slice(SW_TQ * k, SW_TQ * (k + 1))

    def stage_a(slot, n, k):
        ks = window_start(n)
        k_all = jnp.concatenate([k_ref[pl.ds(ks, SW_TK), :], kc_ref[...]], axis=0)
        for g in range(SW_KV):
            in_group = group0 if g == 0 else ~group0
            qs = jnp.concatenate(
                [jnp.where(in_group, q_ref[sub_rows(k), lanes(p)], jnp.zeros((SW_TQ, LANES), BF16))
                 for p in range(SW_GROUP)], axis=0)
            s_scr[slot, g] = _scores(qs, k_all, mask_scr[k])

    def stage_b(slot, n, k):
        for g in range(SW_KV):
            m_scr[slot, g] = _row_max(s_scr[slot, g], sink_rows(g, 1))

    def stage_c(slot, n, k):
        ks = window_start(n - ATTN_C_LAG)
        v_all = jnp.concatenate([v_ref[pl.ds(ks, SW_TK), :], vc_ref[...]], axis=0)
        for g in range(SW_KV):
            m = m_scr[slot, g]
            pv = _exp_pv(s_scr.at[slot, g], m, v_all)
            pv_scr[slot, g, :, :LANES] = pv[:, :LANES]
            pv_scr[slot, g, :, LANES:] = pv[:, LANES:] + jnp.exp(sink_rows(g, LANES) - m)

    def stage_d(slot, n, k):
        outs = [pv_scr[slot, g, :, :LANES] / pv_scr[slot, g, :, LANES:] for g in range(SW_KV)]
        for p in range(SW_GROUP):
            blk = slice(SW_TQ * p, SW_TQ * (p + 1))
            o_ref[sub_rows(k), lanes(p)] = jnp.where(low_half, outs[0][blk], outs[1][blk]).astype(o_ref.dtype)

    _pipeline_step(step, stage_a, stage_b, stage_c, stage_d)


def _swa_attention(sink, qb, kb, vb):
    n_blocks = SW_STEPS
    rows = SW_GROUP * SW_TQ
    scratch = [
        pltpu.VMEM((ATTN_SUB, rows, SW_TK), F32),
        pltpu.VMEM((ATTN_SLOTS, SW_KV, rows, SW_NK), F32),
        pltpu.VMEM((ATTN_SLOTS, SW_KV, rows, LANES), F32),
        pltpu.VMEM((ATTN_SLOTS, SW_KV, rows, 2 * LANES), F32),
    ]
    return _attn_call(_swa_kernel, "swa_attn", (sink, qb, kb, vb, kb, vb),
                      [pl.BlockSpec(memory_space=pltpu.SMEM)], [_batch_spec(SEQ, LANES)] * 2,
                      LANES, SW_TQ, n_blocks, scratch)


def _ctx_attn_kernel(sink_ref, qa_ref, ka_ref, va_ref, qb_ref, kb_ref, vb_ref, oa_ref, ob_ref):
    lane = _lane_iota()
    low_half = lane < HEAD_DIM
    group0 = (lane % HEAD_DIM) < HEAD_DIM // 2

    def lanes(p):
        return slice(LANES * p, LANES * (p + 1))

    def attend(qs, k, v, sink):
        s = _dot_nt(qs, k)
        m = _row_max(s, sink)
        e = jnp.concatenate([jnp.exp(s[:, LANES * c:LANES * (c + 1)] - m) for c in range(CTX_LEN // LANES)],
                            axis=1).astype(BF16)
        pv = _dot(e, jnp.concatenate([v, jnp.ones_like(v)], axis=1))
        den = pv[:, LANES:]
        if sink is not None:
            den = den + jnp.exp(sink - m)
        return pv[:, :LANES] / den

    for p in range(NA_PAIRS):
        qp = qa_ref[:, lanes(p)]
        qs = jnp.concatenate([jnp.where(low_half, qp, jnp.zeros_like(qp)),
                              jnp.where(low_half, jnp.zeros_like(qp), qp)], axis=0)
        out = attend(qs, ka_ref[:, lanes(p)], va_ref[:, lanes(p)], None)
        oa_ref[:, lanes(p)] = jnp.where(low_half, out[:CTX_LEN], out[CTX_LEN:]).astype(oa_ref.dtype)

    outs = []
    for g in range(SW_KV):
        in_group = group0 if g == 0 else ~group0
        qs = jnp.concatenate(
            [jnp.where(in_group, qb_ref[:, lanes(p)], jnp.zeros((CTX_LEN, LANES), BF16))
             for p in range(SW_GROUP)], axis=0)
        sink = jnp.concatenate(
            [jnp.full((CTX_LEN, 1), sink_ref[p + SW_GROUP * g], F32) for p in range(SW_GROUP)], axis=0)
        outs.append(attend(qs, kb_ref[...], vb_ref[...], sink))
    for p in range(SW_GROUP):
        blk = slice(CTX_LEN * p, CTX_LEN * (p + 1))
        ob_ref[:, lanes(p)] = jnp.where(low_half, outs[0][blk], outs[1][blk]).astype(ob_ref.dtype)


def _ctx_attention(sink, qa, ka, va, qb, kb, vb):
    width = NA_HEADS * HEAD_DIM
    ctx0 = N_LAT // CTX_LEN

    def ctx_rows(w):
        return pl.BlockSpec((CTX_LEN, w), lambda b: (ctx0 + b, 0))

    return pl.pallas_call(
        _ctx_attn_kernel,
        grid=(BATCH,),
        in_specs=[pl.BlockSpec(memory_space=pltpu.SMEM),
                  ctx_rows(width), ctx_rows(width), ctx_rows(width),
                  ctx_rows(width), ctx_rows(LANES), ctx_rows(LANES)],
        out_specs=[pl.BlockSpec((CTX_LEN, width), lambda b: (b, 0))] * 2,
        out_shape=[jax.ShapeDtypeStruct((N_CTX, width), BF16)] * 2,
        compiler_params=_cparams(1),
        name="ctx_attn",
    )(sink, qa, ka, va, qb, kb, vb)


HALO = 16


def _merge_kernel(*refs, n_x, n_ab):
    x_parts, a_parts, b_parts = refs[:n_x], refs[n_x:n_x + n_ab], refs[n_x + n_ab:n_x + 2 * n_ab]
    (u_ref, up_ref, un_ref, cgb_ref, gate_ref, g1_ref, cw_ref,
     wa_ref, wb_ref, wc_ref, wo_ref, o_ref) = refs[n_x + 2 * n_ab:]
    i = pl.program_id(0)
    local = lax.broadcasted_iota(jnp.int32, (MERGE_TM, 1), 0)
    row = i * MERGE_TM + local
    is_lat = row < N_LAT
    pos = jnp.where(is_lat, row & (SEQ - 1), (row - N_LAT) & (CTX_LEN - 1))
    seq_first = pos == 0
    seq_last = pos == jnp.where(is_lat, SEQ - 1, CTX_LEN - 1)

    u = u_ref[...].astype(F32)
    prev_row = up_ref[...].astype(F32)[HALO - 1:HALO, :]
    next_row = un_ref[...].astype(F32)[0:1, :]
    u_prev = jnp.where(local == 0, prev_row, pltpu.roll(u, 1, 0))
    u_prev = jnp.where(seq_first, 0.0, u_prev)
    u_next = jnp.where(local == MERGE_TM - 1, next_row, pltpu.roll(u, MERGE_TM - 1, 0))
    u_next = jnp.where(seq_last, 0.0, u_next)
    cw = cw_ref[...]
    y = cw[0:1, :] * u_prev + cw[1:2, :] * u + cw[2:3, :] * u_next
    c_out = (cgb_ref[...].astype(F32) * y).astype(BF16)

    merged = gate_ref[:, 0:D_MODEL].astype(F32) * _dot(_load_parts(a_parts), wa_ref[...])
    merged = merged + gate_ref[:, D_MODEL:2 * D_MODEL].astype(F32) * _dot(_load_parts(b_parts), wb_ref[...])
    merged = merged + gate_ref[:, 2 * D_MODEL:3 * D_MODEL].astype(F32) * _dot(c_out, wc_ref[...])
    o_ref[...] = _load_parts(x_parts) + g1_ref[...] * _dot(merged.astype(BF16), wo_ref[...])


def _merge(xs, a_parts, b_parts, u, cgb, gates, mods, conv_w, wa, wb, wc, wo, n_rows, layer):
    halo_blocks = N_TOK // HALO
    per = MERGE_TM // HALO
    width = NA_HEADS * HEAD_DIM
    ab_specs = _token_specs(len(a_parts) == 2, MERGE_TM, width)

    def tok(wd):
        return pl.BlockSpec((MERGE_TM, wd), lambda i: (i, 0))

    return pl.pallas_call(
        functools.partial(_merge_kernel, n_x=len(xs), n_ab=len(a_parts)),
        grid=(n_rows // MERGE_TM,),
        in_specs=_token_specs(len(xs) == 2, MERGE_TM) + ab_specs + ab_specs + [
            tok(CONV_WIDTH),
            pl.BlockSpec((HALO, CONV_WIDTH), lambda i: (jnp.maximum(i * per - 1, 0), 0)),
            pl.BlockSpec((HALO, CONV_WIDTH), lambda i: (jnp.minimum((i + 1) * per, halo_blocks - 1), 0)),
            tok(CONV_WIDTH), tok(3 * D_MODEL),
            _mod_spec(2, MERGE_TM),
            _const_spec((3, CONV_WIDTH)),
            _layer_spec((512, D_MODEL), layer), _layer_spec((512, D_MODEL), layer),
            _layer_spec((512, D_MODEL), layer), _layer_spec((D_MODEL, D_MODEL), layer),
        ],
        out_specs=tok(D_MODEL),
        out_shape=jax.ShapeDtypeStruct((n_rows, D_MODEL), F32),
        compiler_params=_cparams(1),
        name="merge",
    )(*xs, *a_parts, *b_parts, u, u, u, cgb, gates, mods, conv_w, wa, wb, wc, wo)


FF_CHUNKS = tuple((c, min(c + 512, D_FF)) for c in range(0, D_FF, 512))


def _ffn_kernel(x_ref, sh_ref, sc_ref, g2_ref, ng_ref, fg_ref, wu_ref, wd_ref, o_ref, *, final):
    x = x_ref[...]
    h = _mod_norm(x, ng_ref[...], sh_ref[...], sc_ref[...]).astype(BF16)
    acc = None
    for c0, c1 in FF_CHUNKS:
        gate = _dot(h, wu_ref[:, c0:c1])
        up = _dot(h, wu_ref[:, D_FF + c0:D_FF + c1])
        act = (gate * _sigmoid(gate) * up).astype(BF16)
        part = _dot(act, wd_ref[c0:c1, :])
        acc = part if acc is None else acc + part
    xn = x + g2_ref[...] * acc
    if final:
        xn = xn * lax.rsqrt(jnp.mean(xn * xn, axis=-1, keepdims=True) + NORM_EPS) * fg_ref[...]
    o_ref[...] = xn


def _ffn(xf, mods, ng, fg, wu, wd, final, layer):
    n_rows = xf.shape[0]
    return pl.pallas_call(
        functools.partial(_ffn_kernel, final=final),
        grid=(n_rows // FFN_TM,),
        in_specs=[
            pl.BlockSpec((FFN_TM, D_MODEL), lambda i: (i, 0)),
            _mod_spec(3, FFN_TM), _mod_spec(4, FFN_TM), _mod_spec(5, FFN_TM),
            _const_spec((1, D_MODEL)), _const_spec((1, D_MODEL)),
            _layer_spec((D_MODEL, 2 * D_FF), layer), _layer_spec((D_FF, D_MODEL), layer),
        ],
        out_specs=pl.BlockSpec((FFN_TM, D_MODEL), lambda i: (i, 0)),
        out_shape=jax.ShapeDtypeStruct((n_rows, D_MODEL), F32),
        compiler_params=_cparams(1),
        name="ffn",
    )(xf, mods, mods, mods, ng, fg, wu, wd)


def _prep_w_rot(w_in):
    half = HEAD_DIM // 2
    lead = w_in.shape[:-1]
    qb = w_in[..., C_QB:C_KB].reshape(*lead, 2, SW_GROUP, 2, half)
    qb = jnp.moveaxis(qb, -4, -2).reshape(*lead, SW_HEADS * HEAD_DIM)
    kb = w_in[..., C_KB:C_VB].reshape(*lead, 2, 2, half)
    kb = jnp.swapaxes(kb, -3, -2).reshape(*lead, 2 * HEAD_DIM)
    return jnp.concatenate([qb, kb, w_in[..., C_VB:C_CX]], axis=-1).astype(BF16)


def _prep_w_br_b(w_br_b):
    w = w_br_b.reshape(DEPTH, 2, SW_GROUP, HEAD_DIM, D_MODEL)
    return jnp.swapaxes(w, 1, 2).reshape(DEPTH, SW_HEADS * HEAD_DIM, D_MODEL).astype(BF16)


def _rope_tables():
    t = np.arange(SEQ)
    row = (t // GRID_W).astype(np.float32)
    col = (t % GRID_W).astype(np.float32)
    inv = jnp.asarray(ROPE_THETA, F32) ** (-jnp.arange(ROPE_FREQS, dtype=F32) / ROPE_FREQS)
    ang = jnp.concatenate([jnp.asarray(row)[:, None] * inv, jnp.asarray(col)[:, None] * inv], axis=-1)
    cos = jnp.cos(ang)
    sin = jnp.sin(ang)
    cos4 = jnp.concatenate([cos, cos, cos, cos], axis=-1)
    sin4 = jnp.concatenate([-sin, -sin, sin, sin], axis=-1)
    cos4 = jnp.concatenate([cos4, jnp.ones((TM, LANES), F32)], axis=0)
    sin4 = jnp.concatenate([sin4, jnp.zeros((TM, LANES), F32)], axis=0)
    return cos4, sin4


def _bias_tiles(rpb):
    qc = np.arange(GRID_W)[:, None]
    kc = np.arange(GRID_W)[None, :]
    start = np.clip(qc - NA_WIN_COLS // 2, 0, GRID_W - NA_WIN_COLS)
    valid = (kc >= start) & (kc < start + NA_WIN_COLS)
    dcol = kc - qc + NA_WIN_COLS - 1
    onehot = ((dcol[None] == np.arange(2 * NA_WIN_COLS - 1)[:, None, None]) & valid[None]).astype(np.float32)
    tiles = jnp.einsum("lhrd,dqk->lrhqk", rpb.astype(F32), jnp.asarray(onehot),
                       precision=lax.Precision.HIGHEST)
    return tiles + jnp.asarray(np.where(valid, 0.0, NEG).astype(np.float32))


def kernel(x, c, ctx, c_ctx, w_ada, b_ada, norm1_g, norm2_g, w_in, b_gate, na_rpb, sw_sink, conv_w,
           w_br_a, w_br_b, w_br_c, w_o, w_ffn_up, w_ffn_down, final_norm_g):
    xs = (x.reshape(N_LAT, D_MODEL), ctx.reshape(N_CTX, D_MODEL))
    cvecs = jnp.concatenate([c, c_ctx[None, :], jnp.zeros((8 - BATCH - 1, D_MODEL), F32)], axis=0)
    mods_all = _ada(cvecs, w_ada, b_ada).reshape(DEPTH, 8, 1, 6 * D_MODEL)

    rope_c, rope_s = _rope_tables()
    final_g = final_norm_g.reshape(1, D_MODEL)
    w_in_b, w_rot_b = w_in.astype(BF16), _prep_w_rot(w_in)
    w_a_b, w_b_b, w_c_b = w_br_a.astype(BF16), _prep_w_br_b(w_br_b), w_br_c.astype(BF16)
    w_o_b, w_up_b, w_dn_b = w_o.astype(BF16), w_ffn_up.astype(BF16), w_ffn_down.astype(BF16)
    btab = _bias_tiles(na_rpb)

    for l in range(DEPTH):
        last = l == DEPTH - 1
        mods = mods_all[l]
        qa, ka, va, qb, kb, vb, u, cgb, gates = _inproj(
            xs, mods, norm1_g[l].reshape(1, D_MODEL), w_in_b, w_rot_b,
            b_gate[l].reshape(1, 3 * D_MODEL), rope_c, rope_s, l)
        a_parts = (_na_attention(qa, ka, va, btab, l),)
        b_parts = (_swa_attention(sw_sink[l], qb, kb, vb),)
        if not last:
            a_ctx, b_ctx = _ctx_attention(sw_sink[l], qa, ka, va, qb, kb, vb)
            a_parts, b_parts = a_parts + (a_ctx,), b_parts + (b_ctx,)
        n_rows = N_LAT if last else N_TOK
        xf = _merge(xs, a_parts, b_parts, u, cgb, gates, mods, conv_w[l],
                    w_a_b, w_b_b, w_c_b, w_o_b, n_rows, l)
        xf = _ffn(xf, mods, norm2_g[l].reshape(1, D_MODEL), final_g, w_up_b, w_dn_b, last, l)
        xs = (xf,)
    return xf.reshape(BATCH, SEQ, D_MODEL)
```

```python
import functools

import numpy as np
import jax
import jax.numpy as jnp
from jax import lax
from jax.experimental import pallas as pl
from jax.experimental.pallas import tpu as pltpu

F32 = jnp.float32
BF16 = jnp.bfloat16

D_MODEL = 1024
BATCH = 4
SEQ = 8192
DEPTH = 4
GRID_W = 64
GRID_ROWS = SEQ // GRID_W
CTX_LEN = 256
HEAD_DIM = 64
NORM_EPS = 1e-6
NA_HEADS = 8
NA_WIN_ROWS = 8
NA_WIN_COLS = 16
SW_HEADS = 8
SW_WINDOW = 128
ROPE_THETA = 10000.0
ROPE_FREQS = HEAD_DIM // 4
CONV_WIDTH = 512
D_FF = 2816
N_IN = 6912
N_ROT = 768

N_LAT = BATCH * SEQ
N_CTX = BATCH * CTX_LEN
N_TOK = N_LAT + N_CTX

LANES = 128
HW = NA_HEADS * HEAD_DIM
NEG = -1e30

C_QA, C_KA, C_VA, C_QB, C_KB, C_VB, C_CX, C_CGB, C_CGC, C_ZG = (
    0, 512, 1024, 1536, 2048, 2176, 2304, 2816, 3328, 3840)

TM = 512
MERGE_TM = 1024
FFN_TM = 1024
NA_QROWS = 2
NA_KROWS = NA_QROWS + NA_WIN_ROWS
NA_TQ = NA_QROWS * GRID_W
NA_TK = NA_KROWS * GRID_W
NA_STEPS = GRID_ROWS // NA_QROWS
SW_TQ = 128
SW_TK = 3 * SW_TQ
SW_STEPS = SEQ // SW_TQ
SW_GROUP = 4
VMEM_LIMIT = 56 * 1024 * 1024


def _cparams(n_axes):
    return pltpu.CompilerParams(dimension_semantics=("arbitrary",) * n_axes,
                                vmem_limit_bytes=VMEM_LIMIT)


def _const_spec(shape):
    nd = len(shape)
    return pl.BlockSpec(shape, lambda *_: (0,) * nd, pipeline_mode=pl.Buffered(1))


def _layer_spec(shape, layer):
    nd = len(shape)
    return pl.BlockSpec((None,) + tuple(shape), lambda *_: (layer,) + (0,) * nd, pipeline_mode=pl.Buffered(1))


def _dot(a, b):
    return jnp.dot(a, b, preferred_element_type=F32)


def _dot_nt(a, b):
    return lax.dot_general(a, b, (((1,), (1,)), ((), ())), preferred_element_type=F32)


def _sigmoid(x):
    return 0.5 * jnp.tanh(0.5 * x) + 0.5


def _mod_norm(x, g, shift, scale):
    y = x * lax.rsqrt(jnp.mean(x * x, axis=-1, keepdims=True) + NORM_EPS) * g
    return y * (1.0 + scale) + shift


def _ada_kernel(c_ref, w_ref, b_ref, o_ref):
    c = c_ref[...]
    sc = c * jax.nn.sigmoid(c)
    o_ref[...] = jnp.dot(sc, w_ref[...], preferred_element_type=F32,
                         precision=lax.Precision.HIGHEST) + b_ref[...]


def _ada(cvecs, w_ada, b_ada):
    tn = 1536
    return pl.pallas_call(
        _ada_kernel,
        grid=(DEPTH, 6 * D_MODEL // tn),
        in_specs=[
            pl.BlockSpec((8, D_MODEL), lambda l, n: (0, 0)),
            pl.BlockSpec((None, D_MODEL, tn), lambda l, n: (l, 0, n)),
            pl.BlockSpec((None, 1, tn), lambda l, n: (l, 0, n)),
        ],
        out_specs=pl.BlockSpec((None, 8, tn), lambda l, n: (l, 0, n)),
        out_shape=jax.ShapeDtypeStruct((DEPTH, 8, 6 * D_MODEL), F32),
        compiler_params=_cparams(2),
        name="ada",
    )(cvecs, w_ada, b_ada.reshape(DEPTH, 1, 6 * D_MODEL))


def _mod_row(i, tm):
    lat_blocks = N_LAT // tm
    return jnp.where(i < lat_blocks, i // (SEQ // tm), BATCH)


def _mod_spec(part, tm=TM, block_of=lambda i: i):
    return pl.BlockSpec((None, 1, D_MODEL), lambda i: (_mod_row(block_of(i), tm), 0, part))


def _load_tokens(x_ref, xc_ref):
    if xc_ref is None:
        return x_ref[...]
    return jnp.where(pl.program_id(0) < N_LAT // x_ref.shape[0], x_ref[...], xc_ref[...])


def _load_parts(parts):
    return _load_tokens(parts[0], parts[1] if len(parts) == 2 else None)


def _inproj_kernel(*refs, split):
    n_x = 2 if split else 1
    first_parts, next_parts = refs[:n_x], refs[n_x:2 * n_x]
    (sh0_ref, sc0_ref, shn_ref, scn_ref, g_ref, w_ref, wr_ref, bg_ref, cos_ref, sin_ref,
     qkv_ref, qb_ref, kvb_ref, ucg_ref, gate_ref, h_scr, z0_scr) = refs[2 * n_x:]
    i = pl.program_id(0)
    g = g_ref[...]
    q_scale = HEAD_DIM ** -0.5
    z0_cols = slice(C_ZG, C_ZG + D_MODEL)

    def stage(x, shift, scale, slot):
        h = _mod_norm(x, g, shift, scale).astype(BF16)
        h_scr[slot] = h
        z0_scr[slot] = _dot(h, w_ref[:, z0_cols])

    @pl.when(i == 0)
    def _first():
        stage(first_parts[0][...], sh0_ref[...], sc0_ref[...], 0)

    def body(parity):
        h = h_scr[parity]

        def proj(c0, c1):
            return _dot(h, w_ref[:, c0:c1])

        def gates(k, z):
            cols = slice(D_MODEL * k, D_MODEL * (k + 1))
            gate_ref[:, cols] = _sigmoid(z + bg_ref[:, cols]).astype(BF16)

        gates(0, z0_scr[parity])
        for k in (1, 2):
            gates(k, proj(C_ZG + D_MODEL * k, C_ZG + D_MODEL * (k + 1)))

        cosv = cos_ref[...]
        sinv = sin_ref[...]

        def rope(t):
            return t * cosv + pltpu.roll(t, LANES // 2, 1) * sinv

        rot = _dot(h, wr_ref[...])
        for p in range(4):
            qb_ref[:, LANES * p:LANES * (p + 1)] = (
                rope(rot[:, LANES * p:LANES * (p + 1)]) * q_scale).astype(BF16)
        kvb_ref[:, :LANES] = rope(rot[:, 4 * LANES:5 * LANES]).astype(BF16)
        kvb_ref[:, LANES:] = rot[:, 5 * LANES:].astype(BF16)

        ucg_ref[:, :CONV_WIDTH] = (proj(C_CGC, C_ZG) * proj(C_CX, C_CGB)).astype(BF16)
        qkv_ref[:, :HW] = (proj(C_QA, C_KA) * q_scale).astype(BF16)
        ucg_ref[:, CONV_WIDTH:] = proj(C_CGB, C_CGC).astype(BF16)
        qkv_ref[:, HW:2 * HW] = proj(C_KA, C_VA).astype(BF16)
        qkv_ref[:, 2 * HW:] = proj(C_VA, C_QB).astype(BF16)

        nxt = jnp.minimum(i + 1, pl.num_programs(0) - 1)
        if split:
            x_next = jnp.where(nxt < N_LAT // TM, next_parts[0][...], next_parts[1][...])
        else:
            x_next = next_parts[0][...]
        stage(x_next, shn_ref[...], scn_ref[...], 1 - parity)

    for parity in range(2):
        pl.when(i % 2 == parity)(functools.partial(body, parity))


def _token_specs(split, tm=TM, width=D_MODEL, block_of=lambda i: i):
    lat_blocks = N_LAT // tm
    if not split:
        return [pl.BlockSpec((tm, width), lambda i: (block_of(i), 0))]
    return [pl.BlockSpec((tm, width), lambda i: (jnp.minimum(block_of(i), lat_blocks - 1), 0)),
            pl.BlockSpec((tm, width), lambda i: (jnp.maximum(block_of(i) - lat_blocks, 0), 0),
                         pipeline_mode=pl.Buffered(1))]


def _inproj(xs, mods, g, w, w_rot, bg, rope_c, rope_s, layer):
    lat_blocks = N_LAT // TM
    seq_blocks = SEQ // TM
    split = len(xs) == 2

    def rope_idx(i):
        return (jnp.where(i < lat_blocks, i % seq_blocks, seq_blocks), 0)

    n_blocks = N_TOK // TM

    def first(i):
        return 0 * i

    def nxt(i):
        return jnp.minimum(i + 1, n_blocks - 1)

    widths = (3 * HW, HW, 2 * LANES, 2 * CONV_WIDTH, 3 * D_MODEL)
    return pl.pallas_call(
        functools.partial(_inproj_kernel, split=split),
        grid=(n_blocks,),
        in_specs=_token_specs(split, block_of=first) + _token_specs(split, block_of=nxt) + [
            _mod_spec(0, block_of=first), _mod_spec(1, block_of=first),
            _mod_spec(0, block_of=nxt), _mod_spec(1, block_of=nxt),
            _const_spec((1, D_MODEL)),
            _layer_spec((D_MODEL, N_IN), layer),
            _layer_spec((D_MODEL, N_ROT), layer),
            _const_spec((1, 3 * D_MODEL)),
            pl.BlockSpec((TM, LANES), rope_idx),
            pl.BlockSpec((TM, LANES), rope_idx),
        ],
        out_specs=[pl.BlockSpec((TM, wd), lambda i: (i, 0)) for wd in widths],
        out_shape=[jax.ShapeDtypeStruct((N_TOK, wd), BF16) for wd in widths],
        scratch_shapes=[pltpu.VMEM((2, TM, D_MODEL), BF16), pltpu.VMEM((2, TM, D_MODEL), F32)],
        compiler_params=_cparams(1),
        name="inproj",
    )(*xs, *xs, mods, mods, mods, mods, g, w, w_rot, bg, rope_c, rope_s)


ATTN_SUB = 4
ATTN_SLOTS = 4
ATTN_LAG = 4
ATTN_C_LAG = 2


def _lane_iota():
    return lax.broadcasted_iota(jnp.int32, (1, LANES), 1)


def _pipeline_step(step, stage_a, stage_b, stage_c, stage_d):
    assert ATTN_SUB == ATTN_SLOTS and ATTN_LAG == ATTN_SUB
    for k in range(ATTN_SUB):
        n = ATTN_SUB * step + k
        stage_d(k, n, k)
        stage_c((k - ATTN_C_LAG) % ATTN_SLOTS, n, k)
        stage_b((k - 1) % ATTN_SLOTS, n, k)
        stage_a(k, n, k)


def _init_stage_buffers(s_scr, m_scr, pv_scr):
    s_scr[...] = jnp.zeros_like(s_scr)
    m_scr[...] = jnp.zeros_like(m_scr)
    pv_scr[...] = jnp.ones_like(pv_scr)


def _scores(qs, k_all, bias):
    s = _dot_nt(qs, k_all)
    nb = bias.shape[1]
    return jnp.concatenate([s[:, :nb] + bias, s[:, nb:]], axis=1)


def _row_max(s, sink):
    m = jnp.max(s, axis=-1, keepdims=True)
    if sink is not None:
        m = jnp.maximum(m, sink)
    return jnp.broadcast_to(m, (s.shape[0], LANES))


def _exp_pv(s_ref, m, v_all):
    nk = v_all.shape[0]
    e = jnp.concatenate(
        [jnp.exp(s_ref[:, LANES * c:LANES * (c + 1)] - m) for c in range(nk // LANES)], axis=1).astype(BF16)
    return _dot(e, jnp.concatenate([v_all, jnp.ones_like(v_all)], axis=1))


def _batch_spec(rows, width, col):
    return pl.BlockSpec((rows, width), lambda b, n: (b, col), pipeline_mode=pl.Buffered(1))


def _ctx_spec(width, col):
    return pl.BlockSpec((CTX_LEN, width), lambda b, n: (N_LAT // CTX_LEN + b, col))


def _attn_call(kernel_fn, name, operands, lead_specs, kv_specs, kv_width, kv_cols, tq, n_blocks, scratch):
    width = NA_HEADS * HEAD_DIM
    n_out = N_LAT
    n_steps = n_blocks // ATTN_SUB
    lag_steps = ATTN_LAG // ATTN_SUB
    tq = tq * ATTN_SUB

    def q_idx(b, n):
        return (b * n_steps + jnp.minimum(n, n_steps - 1), 0)

    def o_idx(b, n):
        return (b * n_steps + jnp.maximum(n - lag_steps, 0), 0)

    return pl.pallas_call(
        functools.partial(kernel_fn, n_blocks=n_blocks),
        grid=(BATCH, n_steps + lag_steps),
        in_specs=lead_specs + [
            pl.BlockSpec((tq, width), q_idx),
            *kv_specs,
            _ctx_spec(kv_width, kv_cols[0]), _ctx_spec(kv_width, kv_cols[1]),
        ],
        out_specs=pl.BlockSpec((tq, width), o_idx),
        out_shape=jax.ShapeDtypeStruct((n_out, width), BF16),
        scratch_shapes=scratch,
        compiler_params=_cparams(2),
        name=name,
    )(*operands)


NA_PAIRS = NA_HEADS // 2
NA_NK = NA_TK + CTX_LEN
NA_UROWS = ATTN_SUB * NA_QROWS + NA_WIN_ROWS


def _na_union_row(step, block_lag):
    first_qrow = NA_QROWS * (ATTN_SUB * step - block_lag)
    return jnp.clip(first_qrow - NA_WIN_ROWS // 2, 0, GRID_ROWS - NA_UROWS)


def _na_kernel(bt_ref, q_ref, k_ref, v_ref, kc_ref, vc_ref, o_ref,
               bias_scr, s_scr, m_scr, pv_scr, *, n_blocks):
    step = pl.program_id(1)
    low_half = _lane_iota() < HEAD_DIM

    def window_row(j):
        return jnp.clip(NA_QROWS * j - NA_WIN_ROWS // 2, 0, GRID_ROWS - NA_KROWS)

    def window_start(chain, block_lag):
        kr0 = window_row(jnp.clip(chain, 0, n_blocks - 1))
        union0 = _na_union_row(step, block_lag)
        return pl.multiple_of(jnp.clip(kr0 - union0, 0, NA_UROWS - NA_KROWS) * GRID_W, GRID_W)

    @pl.when((pl.program_id(0) == 0) & (step == 0))
    def _init():
        _init_stage_buffers(s_scr, m_scr, pv_scr)

    def at_edge(j):
        return (j < NA_WIN_ROWS // 2 // NA_QROWS) | (j >= NA_STEPS - NA_WIN_ROWS // 2 // NA_QROWS)

    for k in range(ATTN_SUB):
        jb = ATTN_SUB * step + k

        @pl.when((jb < n_blocks) & ((step == 0) | at_edge(jb) | at_edge(jb - ATTN_SUB)))
        def _regen(k=k, jb=jb):
            kr0 = window_row(jb)
            for i in range(NA_QROWS):
                qr = NA_QROWS * jb + i
                r0 = jnp.clip(qr - NA_WIN_ROWS // 2, 0, GRID_ROWS - NA_WIN_ROWS)
                for jj in range(0, NA_KROWS, 2):
                    tiles = []
                    for kk in (jj, jj + 1):
                        kr = kr0 + kk
                        valid = (kr >= r0) & (kr < r0 + NA_WIN_ROWS)
                        idx = jnp.clip(kr - qr + NA_WIN_ROWS - 1, 0, 2 * NA_WIN_ROWS - 2)
                        tiles.append(jnp.where(valid, bt_ref[idx], NEG))
                    bias_scr[k, :, GRID_W * i:GRID_W * (i + 1), GRID_W * jj:GRID_W * (jj + 2)] = (
                        jnp.concatenate(tiles, axis=-1))

    def lanes(p):
        return slice(LANES * p, LANES * (p + 1))

    def sub_rows(k):
        return slice(NA_TQ * k, NA_TQ * (k + 1))

    def stage_a(slot, n, k):
        ks = window_start(n, 0)
        for p in range(NA_PAIRS):
            qp = q_ref[sub_rows(k), lanes(p)]
            qs = jnp.concatenate([jnp.where(low_half, qp, jnp.zeros_like(qp)),
                                  jnp.where(low_half, jnp.zeros_like(qp), qp)], axis=0)
            k_all = jnp.concatenate([k_ref[pl.ds(ks, NA_TK), lanes(p)], kc_ref[:, lanes(p)]], axis=0)
            s_scr[slot, p] = _scores(qs, k_all, bias_scr[k, 2 * p:2 * p + 2].reshape(2 * NA_TQ, NA_TK))

    def stage_b(slot, n, k):
        for p in range(NA_PAIRS):
            m_scr[slot, p] = _row_max(s_scr[slot, p], None)

    def stage_c(slot, n, k):
        ks = window_start(n - ATTN_C_LAG, ATTN_C_LAG)
        for p in range(NA_PAIRS):
            v_all = jnp.concatenate([v_ref[pl.ds(ks, NA_TK), lanes(p)], vc_ref[:, lanes(p)]], axis=0)
            pv_scr[slot, p] = _exp_pv(s_scr.at[slot, p], m_scr[slot, p], v_all)

    def stage_d(slot, n, k):
        for p in range(NA_PAIRS):
            pv = pv_scr[slot, p]
            out = pv[:, :LANES] / pv[:, LANES:]
            o_ref[sub_rows(k), lanes(p)] = jnp.where(low_half, out[:NA_TQ], out[NA_TQ:]).astype(o_ref.dtype)

    _pipeline_step(step, stage_a, stage_b, stage_c, stage_d)


def _na_attention(qkv, btab, layer):
    n_blocks = NA_STEPS
    rows = 2 * NA_TQ
    scratch = [
        pltpu.VMEM((ATTN_SUB, NA_HEADS, NA_TQ, NA_TK), F32),
        pltpu.VMEM((ATTN_SLOTS, NA_PAIRS, rows, NA_NK), F32),
        pltpu.VMEM((ATTN_SLOTS, NA_PAIRS, rows, LANES), F32),
        pltpu.VMEM((ATTN_SLOTS, NA_PAIRS, rows, 2 * LANES), F32),
    ]
    bt_spec = [_layer_spec((2 * NA_WIN_ROWS - 1, NA_HEADS, GRID_W, GRID_W), layer)]
    width = NA_HEADS * HEAD_DIM

    def window_spec(block_lag, col):
        return pl.BlockSpec(
            (pl.Element(NA_UROWS * GRID_W), pl.Element(width)),
            lambda b, s: (pl.multiple_of(b * SEQ + _na_union_row(s, block_lag) * GRID_W, GRID_W), col * width))

    return _attn_call(_na_kernel, "na_attn", (btab, qkv, qkv, qkv, qkv, qkv), bt_spec,
                      [window_spec(0, 1), window_spec(ATTN_C_LAG, 2)], width, (1, 2), NA_TQ, n_blocks, scratch)


SW_NK = SW_TK + CTX_LEN
SW_KV = SW_HEADS // SW_GROUP


def _swa_kernel(sink_ref, q_ref, k_ref, v_ref, kc_ref, vc_ref, o_ref,
                mask_scr, s_scr, m_scr, pv_scr, *, n_blocks):
    step = pl.program_id(1)
    lane = _lane_iota()
    low_half = lane < HEAD_DIM
    group0 = (lane % HEAD_DIM) < HEAD_DIM // 2

    def window_row(j):
        return jnp.clip(j * SW_TQ - SW_TQ, 0, SEQ - SW_TK)

    def window_start(chain):
        return pl.multiple_of(window_row(jnp.clip(chain, 0, n_blocks - 1)), SW_TQ)

    def sink_rows(g, width):
        return jnp.concatenate(
            [jnp.full((SW_TQ, width), sink_ref[p + SW_GROUP * g], F32) for p in range(SW_GROUP)], axis=0)

    def lanes(p):
        return slice(LANES * p, LANES * (p + 1))

    @pl.when((pl.program_id(0) == 0) & (step == 0))
    def _init():
        _init_stage_buffers(s_scr, m_scr, pv_scr)

    def at_edge(j):
        return (j == 0) | (j == SW_STEPS - 1)

    for k in range(ATTN_SUB):
        jb = ATTN_SUB * step + k

        @pl.when((jb < n_blocks) & ((step == 0) | at_edge(jb) | at_edge(jb - ATTN_SUB)))
        def _regen(k=k, jb=jb):
            rel = ((window_row(jb) - jb * SW_TQ) + lax.broadcasted_iota(jnp.int32, (SW_TQ, SW_TK), 1)
                   - lax.broadcasted_iota(jnp.int32, (SW_TQ, SW_TK), 0))
            band = jnp.where(jnp.abs(rel) <= SW_WINDOW, 0.0, NEG)
            for p in range(SW_GROUP):
                mask_scr[k, SW_TQ * p:SW_TQ * (p + 1), :] = band

    def sub_rows(k):
        return slice(SW_TQ * k, SW_TQ * (k + 1))

    def stage_a(slot, n, k):
        ks = window_start(n)
        k_all = jnp.concatenate([k_ref[pl.ds(ks, SW_TK), :], kc_ref[...]], axis=0)
        for g in range(SW_KV):
            in_group = group0 if g == 0 else ~group0
            qs = jnp.concatenate(
                [jnp.where(in_group, q_ref[sub_rows(k), lanes(p)], jnp.zeros((SW_TQ, LANES), BF16))
                 for p in range(SW_GROUP)], axis=0)
            s_scr[slot, g] = _scores(qs, k_all, mask_scr[k])

    def stage_b(slot, n, k):
        for g in range(SW_KV):
            m_scr[slot, g] = _row_max(s_scr[slot, g], sink_rows(g, 1))

    def stage_c(slot, n, k):
        ks = window_start(n - ATTN_C_LAG)
        v_all = jnp.concatenate([v_ref[pl.ds(ks, SW_TK), :], vc_ref[...]], axis=0)
        for g in range(SW_KV):
            m = m_scr[slot, g]
            pv = _exp_pv(s_scr.at[slot, g], m, v_all)
            pv_scr[slot, g, :, :LANES] = pv[:, :LANES]
            pv_scr[slot, g, :, LANES:] = pv[:, LANES:] + jnp.exp(sink_rows(g, LANES) - m)

    def stage_d(slot, n, k):
        outs = [pv_scr[slot, g, :, :LANES] / pv_scr[slot, g, :, LANES:] for g in range(SW_KV)]
        for p in range(SW_GROUP):
            blk = slice(SW_TQ * p, SW_TQ * (p + 1))
            o_ref[sub_rows(k), lanes(p)] = jnp.where(low_half, outs[0][blk], outs[1][blk]).astype(o_ref.dtype)

    _pipeline_step(step, stage_a, stage_b, stage_c, stage_d)


def _swa_attention(sink, qb, kvb):
    n_blocks = SW_STEPS
    rows = SW_GROUP * SW_TQ
    scratch = [
        pltpu.VMEM((ATTN_SUB, rows, SW_TK), F32),
        pltpu.VMEM((ATTN_SLOTS, SW_KV, rows, SW_NK), F32),
        pltpu.VMEM((ATTN_SLOTS, SW_KV, rows, LANES), F32),
        pltpu.VMEM((ATTN_SLOTS, SW_KV, rows, 2 * LANES), F32),
    ]
    return _attn_call(_swa_kernel, "swa_attn", (sink, qb, kvb, kvb, kvb, kvb),
                      [pl.BlockSpec(memory_space=pltpu.SMEM)], [_batch_spec(SEQ, LANES, 0), _batch_spec(SEQ, LANES, 1)],
                      LANES, (0, 1), SW_TQ, n_blocks, scratch)


def _ctx_attn_kernel(sink_ref, qa_ref, ka_ref, va_ref, qb_ref, kb_ref, vb_ref, oa_ref, ob_ref):
    lane = _lane_iota()
    low_half = lane < HEAD_DIM
    group0 = (lane % HEAD_DIM) < HEAD_DIM // 2

    def lanes(p):
        return slice(LANES * p, LANES * (p + 1))

    def attend(qs, k, v, sink):
        s = _dot_nt(qs, k)
        m = _row_max(s, sink)
        e = jnp.concatenate([jnp.exp(s[:, LANES * c:LANES * (c + 1)] - m) for c in range(CTX_LEN // LANES)],
                            axis=1).astype(BF16)
        pv = _dot(e, jnp.concatenate([v, jnp.ones_like(v)], axis=1))
        den = pv[:, LANES:]
        if sink is not None:
            den = den + jnp.exp(sink - m)
        return pv[:, :LANES] / den

    for p in range(NA_PAIRS):
        qp = qa_ref[:, lanes(p)]
        qs = jnp.concatenate([jnp.where(low_half, qp, jnp.zeros_like(qp)),
                              jnp.where(low_half, jnp.zeros_like(qp), qp)], axis=0)
        out = attend(qs, ka_ref[:, lanes(p)], va_ref[:, lanes(p)], None)
        oa_ref[:, lanes(p)] = jnp.where(low_half, out[:CTX_LEN], out[CTX_LEN:]).astype(oa_ref.dtype)

    outs = []
    for g in range(SW_KV):
        in_group = group0 if g == 0 else ~group0
        qs = jnp.concatenate(
            [jnp.where(in_group, qb_ref[:, lanes(p)], jnp.zeros((CTX_LEN, LANES), BF16))
             for p in range(SW_GROUP)], axis=0)
        sink = jnp.concatenate(
            [jnp.full((CTX_LEN, 1), sink_ref[p + SW_GROUP * g], F32) for p in range(SW_GROUP)], axis=0)
        outs.append(attend(qs, kb_ref[...], vb_ref[...], sink))
    for p in range(SW_GROUP):
        blk = slice(CTX_LEN * p, CTX_LEN * (p + 1))
        ob_ref[:, lanes(p)] = jnp.where(low_half, outs[0][blk], outs[1][blk]).astype(ob_ref.dtype)


def _ctx_attention(sink, qkv, qb, kvb):
    width = NA_HEADS * HEAD_DIM
    ctx0 = N_LAT // CTX_LEN

    def ctx_rows(w, col=0):
        return pl.BlockSpec((CTX_LEN, w), lambda b: (ctx0 + b, col))

    return pl.pallas_call(
        _ctx_attn_kernel,
        grid=(BATCH,),
        in_specs=[pl.BlockSpec(memory_space=pltpu.SMEM),
                  ctx_rows(width, 0), ctx_rows(width, 1), ctx_rows(width, 2),
                  ctx_rows(width), ctx_rows(LANES, 0), ctx_rows(LANES, 1)],
        out_specs=[pl.BlockSpec((CTX_LEN, width), lambda b: (b, 0))] * 2,
        out_shape=[jax.ShapeDtypeStruct((N_CTX, width), BF16)] * 2,
        compiler_params=_cparams(1),
        name="ctx_attn",
    )(sink, qkv, qkv, qkv, qb, kvb, kvb)


HALO = 16


def _merge_kernel(*refs, n_x, n_ab):
    x_parts, a_parts, b_parts = refs[:n_x], refs[n_x:n_x + n_ab], refs[n_x + n_ab:n_x + 2 * n_ab]
    (u_ref, up_ref, un_ref, cgb_ref, gate_ref, g1_ref, cw_ref,
     wa_ref, wb_ref, wc_ref, wo_ref, o_ref) = refs[n_x + 2 * n_ab:]
    i = pl.program_id(0)
    local = lax.broadcasted_iota(jnp.int32, (MERGE_TM, 1), 0)
    row = i * MERGE_TM + local
    is_lat = row < N_LAT
    pos = jnp.where(is_lat, row & (SEQ - 1), (row - N_LAT) & (CTX_LEN - 1))
    seq_first = pos == 0
    seq_last = pos == jnp.where(is_lat, SEQ - 1, CTX_LEN - 1)

    u = u_ref[...].astype(F32)
    prev_row = up_ref[...].astype(F32)[HALO - 1:HALO, :]
    next_row = un_ref[...].astype(F32)[0:1, :]
    u_prev = jnp.where(local == 0, prev_row, pltpu.roll(u, 1, 0))
    u_prev = jnp.where(seq_first, 0.0, u_prev)
    u_next = jnp.where(local == MERGE_TM - 1, next_row, pltpu.roll(u, MERGE_TM - 1, 0))
    u_next = jnp.where(seq_last, 0.0, u_next)
    cw = cw_ref[...]
    y = cw[0:1, :] * u_prev + cw[1:2, :] * u + cw[2:3, :] * u_next
    c_out = (cgb_ref[...].astype(F32) * y).astype(BF16)

    merged = gate_ref[:, 0:D_MODEL].astype(F32) * _dot(_load_parts(a_parts), wa_ref[...])
    merged = merged + gate_ref[:, D_MODEL:2 * D_MODEL].astype(F32) * _dot(_load_parts(b_parts), wb_ref[...])
    merged = merged + gate_ref[:, 2 * D_MODEL:3 * D_MODEL].astype(F32) * _dot(c_out, wc_ref[...])
    o_ref[...] = _load_parts(x_parts) + g1_ref[...] * _dot(merged.astype(BF16), wo_ref[...])


def _merge(xs, a_parts, b_parts, ucg, gates, mods, conv_w, wa, wb, wc, wo, n_rows, layer):
    halo_blocks = N_TOK // HALO
    per = MERGE_TM // HALO
    width = NA_HEADS * HEAD_DIM
    ab_specs = _token_specs(len(a_parts) == 2, MERGE_TM, width)

    def tok(wd, col=0):
        return pl.BlockSpec((MERGE_TM, wd), lambda i: (i, col))

    return pl.pallas_call(
        functools.partial(_merge_kernel, n_x=len(xs), n_ab=len(a_parts)),
        grid=(n_rows // MERGE_TM,),
        in_specs=_token_specs(len(xs) == 2, MERGE_TM) + ab_specs + ab_specs + [
            tok(CONV_WIDTH),
            pl.BlockSpec((HALO, CONV_WIDTH), lambda i: (jnp.maximum(i * per - 1, 0), 0)),
            pl.BlockSpec((HALO, CONV_WIDTH), lambda i: (jnp.minimum((i + 1) * per, halo_blocks - 1), 0)),
            tok(CONV_WIDTH, 1), tok(3 * D_MODEL),
            _mod_spec(2, MERGE_TM),
            _const_spec((3, CONV_WIDTH)),
            _layer_spec((512, D_MODEL), layer), _layer_spec((512, D_MODEL), layer),
            _layer_spec((512, D_MODEL), layer), _layer_spec((D_MODEL, D_MODEL), layer),
        ],
        out_specs=tok(D_MODEL),
        out_shape=jax.ShapeDtypeStruct((n_rows, D_MODEL), F32),
        compiler_params=_cparams(1),
        name="merge",
    )(*xs, *a_parts, *b_parts, ucg, ucg, ucg, ucg, gates, mods, conv_w, wa, wb, wc, wo)


FF_CHUNKS = tuple((c, min(c + 512, D_FF)) for c in range(0, D_FF, 512))


def _ffn_kernel(x_ref, sh_ref, sc_ref, g2_ref, ng_ref, fg_ref, wu_ref, wd_ref, o_ref, *, final):
    x = x_ref[...]
    h = _mod_norm(x, ng_ref[...], sh_ref[...], sc_ref[...]).astype(BF16)
    acc = None
    for c0, c1 in FF_CHUNKS:
        gate = _dot(h, wu_ref[:, c0:c1])
        up = _dot(h, wu_ref[:, D_FF + c0:D_FF + c1])
        act = (gate * _sigmoid(gate) * up).astype(BF16)
        part = _dot(act, wd_ref[c0:c1, :])
        acc = part if acc is None else acc + part
    xn = x + g2_ref[...] * acc
    if final:
        xn = xn * lax.rsqrt(jnp.mean(xn * xn, axis=-1, keepdims=True) + NORM_EPS) * fg_ref[...]
    o_ref[...] = xn


def _ffn(xf, mods, ng, fg, wu, wd, final, layer):
    n_rows = xf.shape[0]
    return pl.pallas_call(
        functools.partial(_ffn_kernel, final=final),
        grid=(n_rows // FFN_TM,),
        in_specs=[
            pl.BlockSpec((FFN_TM, D_MODEL), lambda i: (i, 0)),
            _mod_spec(3, FFN_TM), _mod_spec(4, FFN_TM), _mod_spec(5, FFN_TM),
            _const_spec((1, D_MODEL)), _const_spec((1, D_MODEL)),
            _layer_spec((D_MODEL, 2 * D_FF), layer), _layer_spec((D_FF, D_MODEL), layer),
        ],
        out_specs=pl.BlockSpec((FFN_TM, D_MODEL), lambda i: (i, 0)),
        out_shape=jax.ShapeDtypeStruct((n_rows, D_MODEL), F32),
        compiler_params=_cparams(1),
        name="ffn",
    )(xf, mods, mods, mods, ng, fg, wu, wd)


def _prep_w_rot(w_in):
    half = HEAD_DIM // 2
    lead = w_in.shape[:-1]
    qb = w_in[..., C_QB:C_KB].reshape(*lead, 2, SW_GROUP, 2, half)
    qb = jnp.moveaxis(qb, -4, -2).reshape(*lead, SW_HEADS * HEAD_DIM)
    kb = w_in[..., C_KB:C_VB].reshape(*lead, 2, 2, half)
    kb = jnp.swapaxes(kb, -3, -2).reshape(*lead, 2 * HEAD_DIM)
    return jnp.concatenate([qb, kb, w_in[..., C_VB:C_CX]], axis=-1).astype(BF16)


def _prep_w_br_b(w_br_b):
    w = w_br_b.reshape(DEPTH, 2, SW_GROUP, HEAD_DIM, D_MODEL)
    return jnp.swapaxes(w, 1, 2).reshape(DEPTH, SW_HEADS * HEAD_DIM, D_MODEL).astype(BF16)


def _rope_tables():
    t = np.arange(SEQ)
    row = (t // GRID_W).astype(np.float32)
    col = (t % GRID_W).astype(np.float32)
    inv = jnp.asarray(ROPE_THETA, F32) ** (-jnp.arange(ROPE_FREQS, dtype=F32) / ROPE_FREQS)
    ang = jnp.concatenate([jnp.asarray(row)[:, None] * inv, jnp.asarray(col)[:, None] * inv], axis=-1)
    cos = jnp.cos(ang)
    sin = jnp.sin(ang)
    cos4 = jnp.concatenate([cos, cos, cos, cos], axis=-1)
    sin4 = jnp.concatenate([-sin, -sin, sin, sin], axis=-1)
    cos4 = jnp.concatenate([cos4, jnp.ones((TM, LANES), F32)], axis=0)
    sin4 = jnp.concatenate([sin4, jnp.zeros((TM, LANES), F32)], axis=0)
    return cos4, sin4


def _bias_tiles(rpb):
    qc = np.arange(GRID_W)[:, None]
    kc = np.arange(GRID_W)[None, :]
    start = np.clip(qc - NA_WIN_COLS // 2, 0, GRID_W - NA_WIN_COLS)
    valid = (kc >= start) & (kc < start + NA_WIN_COLS)
    dcol = kc - qc + NA_WIN_COLS - 1
    onehot = ((dcol[None] == np.arange(2 * NA_WIN_COLS - 1)[:, None, None]) & valid[None]).astype(np.float32)
    tiles = jnp.einsum("lhrd,dqk->lrhqk", rpb.astype(F32), jnp.asarray(onehot),
                       precision=lax.Precision.HIGHEST)
    return tiles + jnp.asarray(np.where(valid, 0.0, NEG).astype(np.float32))


def kernel(x, c, ctx, c_ctx, w_ada, b_ada, norm1_g, norm2_g, w_in, b_gate, na_rpb, sw_sink, conv_w,
           w_br_a, w_br_b, w_br_c, w_o, w_ffn_up, w_ffn_down, final_norm_g):
    xs = (x.reshape(N_LAT, D_MODEL), ctx.reshape(N_CTX, D_MODEL))
    cvecs = jnp.concatenate([c, c_ctx[None, :], jnp.zeros((8 - BATCH - 1, D_MODEL), F32)], axis=0)
    mods_all = _ada(cvecs, w_ada, b_ada).reshape(DEPTH, 8, 1, 6 * D_MODEL)

    rope_c, rope_s = _rope_tables()
    final_g = final_norm_g.reshape(1, D_MODEL)
    w_in_b, w_rot_b = w_in.astype(BF16), _prep_w_rot(w_in)
    w_a_b, w_b_b, w_c_b = w_br_a.astype(BF16), _prep_w_br_b(w_br_b), w_br_c.astype(BF16)
    w_o_b, w_up_b, w_dn_b = w_o.astype(BF16), w_ffn_up.astype(BF16), w_ffn_down.astype(BF16)
    btab = _bias_tiles(na_rpb)

    for l in range(DEPTH):
        last = l == DEPTH - 1
        mods = mods_all[l]
        qkv, qb, kvb, ucg, gates = _inproj(
            xs, mods, norm1_g[l].reshape(1, D_MODEL), w_in_b, w_rot_b,
            b_gate[l].reshape(1, 3 * D_MODEL), rope_c, rope_s, l)
        a_parts = (_na_attention(qkv, btab, l),)
        b_parts = (_swa_attention(sw_sink[l], qb, kvb),)
        if not last:
            a_ctx, b_ctx = _ctx_attention(sw_sink[l], qkv, qb, kvb)
            a_parts, b_parts = a_parts + (a_ctx,), b_parts + (b_ctx,)
        n_rows = N_LAT if last else N_TOK
        xf = _merge(xs, a_parts, b_parts, ucg, gates, mods, conv_w[l],
                    w_a_b, w_b_b, w_c_b, w_o_b, n_rows, l)
        xf = _ffn(xf, mods, norm2_g[l].reshape(1, D_MODEL), final_g, w_up_b, w_dn_b, last, l)
        xs = (xf,)
    return xf.reshape(BATCH, SEQ, D_MODEL)
```
